```python
import math
import jax, jax.numpy as jnp
from jax import lax
import numpy as np

D_MODEL = 1024
BATCH = 2
SEQ = 16384
DEPTH = 2

MIX_WIDTH = D_MODEL
CONV_CH = MIX_WIDTH // 2
CONV_GROUPS = 8
CONV_K = 31
DIFF_WIDTH = MIX_WIDTH - CONV_CH
HEAD_DIM = 64
N_DIFF_HEADS = DIFF_WIDTH // (2 * HEAD_DIM)
D_FF = 2816
NUM_BUCKETS = 32
MAX_DISTANCE = 128
BLOCK_Q = 128
EPS = 1e-6
IN_COLS = 2 * CONV_CH + 3 * DIFF_WIDTH

kernel_name = "hybrid_conformer_conv_diffattn_macaron"


def rms_norm(x, g):
    xf = x.astype(jnp.float32)
    y = xf * lax.rsqrt(jnp.mean(xf * xf, axis=-1, keepdims=True) + EPS)
    return (y * g.astype(jnp.float32)).astype(x.dtype)


def swiglu(h, w_gate, w_up, w_down):
    return (jax.nn.silu(h @ w_gate) * (h @ w_up)) @ w_down


def lambda_init_fn(layer_idx):
    return 0.8 - 0.6 * math.exp(-0.3 * layer_idx)


def rel_bucket(n):
    max_exact = NUM_BUCKETS // 2
    nf = jnp.maximum(n, 1).astype(jnp.float32)
    large = max_exact + (jnp.log(nf / max_exact) / math.log(MAX_DISTANCE / max_exact)
                         * (NUM_BUCKETS - max_exact)).astype(jnp.int32)
    large = jnp.minimum(large, NUM_BUCKETS - 1)
    return jnp.where(n < max_exact, n, large)


def conv_module(a, g, conv_w, conv_b, conv_norm):
    u = a * jax.nn.sigmoid(g)
    rhs = conv_w[:, None, :].astype(u.dtype)
    y = lax.conv_general_dilated(u, rhs, window_strides=(1,), padding=[(CONV_K - 1, 0)],
                                 dimension_numbers=("NWC", "WIO", "NWC"),
                                 feature_group_count=CONV_CH)
    y = y + conv_b
    return jax.nn.silu(rms_norm(y, conv_norm))


def diff_attention(q, k, v, rel_bias, lam, lam_init, subln_g):
    B, H, _, S, d = q.shape
    nb = S // BLOCK_Q
    scale = 1.0 / math.sqrt(d)
    qb = q.reshape(B, H, 2, nb, BLOCK_Q, d).transpose(3, 0, 1, 2, 4, 5)
    k_pos = jnp.arange(S, dtype=jnp.int32)
    bias_tab = rel_bias.astype(jnp.float32)

    def block(args):
        qi, i = args
        q_pos = i * BLOCK_Q + jnp.arange(BLOCK_Q, dtype=jnp.int32)
        dist = q_pos[:, None] - k_pos[None, :]
        bias = bias_tab[rel_bucket(jnp.maximum(dist, 0))].transpose(2, 0, 1)
        s = jnp.einsum("bhmqd,bhmkd->bhmqk", qi, k).astype(jnp.float32) * scale
        s = s + bias[None, :, None]
        s = jnp.where((dist >= 0)[None, None, None], s, -1e30)
        p = jax.nn.softmax(s, axis=-1)
        a = p[:, :, 0] - lam * p[:, :, 1]
        return jnp.einsum("bhqk,bhkd->bhqd", a.astype(v.dtype), v)

    o = lax.map(block, (qb, jnp.arange(nb, dtype=jnp.int32)))
    o = o.transpose(1, 0, 3, 2, 4).reshape(B, S, H, 2 * d)
    o = rms_norm(o, subln_g) * (1.0 - lam_init)
    return o.reshape(B, S, H * 2 * d)


def setup_inputs(seed: int = 0) -> dict:
    key = jax.random.key(seed)
    ks = jax.random.split(key, 24)
    f32 = jnp.float32

    def nrm(k, shape, scale):
        return jax.random.normal(k, shape, f32) * scale

    def gain(k, shape):
        return 1.0 + 0.02 * jax.random.normal(k, shape, f32)

    L = DEPTH
    return {
        "x": jax.random.normal(ks[0], (BATCH, SEQ, D_MODEL), f32),
        "rel_bias": nrm(ks[1], (NUM_BUCKETS, N_DIFF_HEADS), 0.5),
        "ffn1_norm": gain(ks[2], (L, D_MODEL)),
        "ffn1_w_gate": nrm(ks[3], (L, D_MODEL, D_FF), D_MODEL ** -0.5),
        "ffn1_w_up": nrm(ks[4], (L, D_MODEL, D_FF), D_MODEL ** -0.5),
        "ffn1_w_down": nrm(ks[5], (L, D_FF, D_MODEL), D_FF ** -0.5),
        "mix_norm": gain(ks[6], (L, D_MODEL)),
        "w_in": nrm(ks[7], (L, D_MODEL, IN_COLS), D_MODEL ** -0.5),
        "conv_w": nrm(ks[8], (L, CONV_K, CONV_CH), CONV_K ** -0.5),
        "conv_b": nrm(ks[9], (L, CONV_CH), 0.02),
        "conv_norm": gain(ks[10], (L, CONV_CH)),
        "q_norm": gain(ks[11], (L, HEAD_DIM)),
        "k_norm": gain(ks[12], (L, HEAD_DIM)),
        "lambda_q1": nrm(ks[13], (L, HEAD_DIM), 0.1),
        "lambda_k1": nrm(ks[14], (L, HEAD_DIM), 0.1),
        "lambda_q2": nrm(ks[15], (L, HEAD_DIM), 0.1),
        "lambda_k2": nrm(ks[16], (L, HEAD_DIM), 0.1),
        "subln_norm": gain(ks[17], (L, 2 * HEAD_DIM)),
        "w_out": nrm(ks[18], (L, MIX_WIDTH, D_MODEL), MIX_WIDTH ** -0.5),
        "ffn2_norm": gain(ks[19], (L, D_MODEL)),
        "ffn2_w_gate": nrm(ks[20], (L, D_MODEL, D_FF), D_MODEL ** -0.5),
        "ffn2_w_up": nrm(ks[21], (L, D_MODEL, D_FF), D_MODEL ** -0.5),
        "ffn2_w_down": nrm(ks[22], (L, D_FF, D_MODEL), D_FF ** -0.5),
    }


def reference(x, rel_bias, ffn1_norm, ffn1_w_gate, ffn1_w_up, ffn1_w_down, mix_norm, w_in,
              conv_w, conv_b, conv_norm, q_norm, k_norm, lambda_q1, lambda_k1, lambda_q2,
              lambda_k2, subln_norm, w_out, ffn2_norm, ffn2_w_gate, ffn2_w_up, ffn2_w_down):
    B, S, _ = x.shape
    H, d = N_DIFF_HEADS, HEAD_DIM
    splits = np.cumsum([CONV_CH, CONV_CH, DIFF_WIDTH, DIFF_WIDTH]).tolist()
    for l in range(DEPTH):
        x = x + 0.5 * swiglu(rms_norm(x, ffn1_norm[l]), ffn1_w_gate[l], ffn1_w_up[l], ffn1_w_down[l])

        h = rms_norm(x, mix_norm[l])
        a, g, q, k, v = jnp.split(h @ w_in[l], splits, axis=-1)

        conv_out = conv_module(a, g, conv_w[l], conv_b[l], conv_norm[l])

        q = q.reshape(B, S, H, 2, d).transpose(0, 2, 3, 1, 4)
        k = k.reshape(B, S, H, 2, d).transpose(0, 2, 3, 1, 4)
        q = rms_norm(q, q_norm[l])
        k = rms_norm(k, k_norm[l])
        v = v.reshape(B, S, H, 2 * d).transpose(0, 2, 1, 3)
        lam_init = lambda_init_fn(l)
        lam = (jnp.exp(jnp.sum(lambda_q1[l].astype(jnp.float32) * lambda_k1[l].astype(jnp.float32)))
               - jnp.exp(jnp.sum(lambda_q2[l].astype(jnp.float32) * lambda_k2[l].astype(jnp.float32)))
               + lam_init)
        diff_out = diff_attention(q, k, v, rel_bias, lam, lam_init, subln_norm[l])

        x = x + jnp.concatenate([conv_out, diff_out], axis=-1) @ w_out[l]

        x = x + 0.5 * swiglu(rms_norm(x, ffn2_norm[l]), ffn2_w_gate[l], ffn2_w_up[l], ffn2_w_down[l])
    return x
```

```python
import functools
import math

import jax
import jax.numpy as jnp
import numpy as np
from jax import lax
from jax.experimental import pallas as pl
from jax.experimental.pallas import tpu as pltpu

F32 = jnp.float32
BF16 = jnp.bfloat16

D_MODEL = 1024
CONV_CH = 512
CONV_K = 31
DIFF_WIDTH = 512
HEAD_DIM = 64
N_HEADS = 4
D_FF = 2816
NUM_BUCKETS = 32
MAX_DISTANCE = 128
EPS = 1e-6
LOG2E = 1.4426950408889634
NEG_BIG = -1e30

FF_CHUNK = 256
N_FF_CHUNKS = D_FF // FF_CHUNK
ROW_TILE = 512
ATT_TILE = 512
CONV_HALO = 32
CONV_ROWS = 64
GROUP_MM = 256
VMEM_LIMIT = 56 * 1024 * 1024


def _rms(x, g):
    ms = jnp.mean(x * x, axis=-1, keepdims=True)
    return x * lax.rsqrt(ms + EPS) * g


def _ffn(x, g_ref, wgu_ref, wd_ref, act_ref):
    h = _rms(x, g_ref[...]).astype(BF16)
    for c in range(N_FF_CHUNKS):
        gu = jnp.dot(h, wgu_ref[c], preferred_element_type=F32)
        gate = gu[:, :FF_CHUNK]
        up = gu[:, FF_CHUNK:]
        act_ref[:, c * FF_CHUNK:(c + 1) * FF_CHUNK] = (gate * jax.nn.sigmoid(gate) * up).astype(BF16)
    y = jnp.dot(act_ref[...], wd_ref[...], preferred_element_type=F32)
    return x + 0.5 * y


def _ffn_proj_body(x_ref, n1_ref, wgu_ref, wd_ref, nm_ref, win_ref, gsum_ref, qkg_ref,
                   xo_ref, u_ref, q_ref, k_ref, v_ref, act_ref):
    x1 = _ffn(x_ref[...], n1_ref, wgu_ref, wd_ref, act_ref)
    xo_ref[...] = x1
    h = _rms(x1, nm_ref[...]).astype(BF16)
    ag = jnp.dot(h, win_ref[:, 0:2 * CONV_CH], preferred_element_type=F32)
    u_ref[...] = ag[:, :CONV_CH] * jax.nn.sigmoid(ag[:, CONV_CH:])
    qk = jnp.dot(h, win_ref[:, 2 * CONV_CH:2 * CONV_CH + 2 * DIFF_WIDTH], preferred_element_type=F32)
    sq = (qk * qk).astype(BF16)
    ss = jnp.concatenate(
        [jnp.dot(sq[:, c * GROUP_MM:(c + 1) * GROUP_MM], gsum_ref[...], preferred_element_type=F32)
         for c in range(2 * DIFF_WIDTH // GROUP_MM)], axis=1)
    qkn = qk * lax.rsqrt(ss * (1.0 / HEAD_DIM) + EPS) * qkg_ref[...]
    q_ref[...] = qkn[:, :DIFF_WIDTH].astype(BF16)
    k_ref[...] = qkn[:, DIFF_WIDTH:].astype(BF16)
    v = jnp.dot(h, win_ref[:, 2 * CONV_CH + 2 * DIFF_WIDTH:], preferred_element_type=F32)
    v_ref[...] = v.astype(BF16)


def _const_spec(shape):
    nd = len(shape)
    return pl.BlockSpec(shape, lambda i: (0,) * nd, pipeline_mode=pl.Buffered(1))


def _ffn_proj(x, n1, wgu, wd, nm, win, gsum, qkg):
    t = x.shape[0]
    tm = ROW_TILE
    row = lambda w: pl.BlockSpec((tm, w), lambda i: (i, 0))
    return pl.pallas_call(
        _ffn_proj_body,
        grid=(t // tm,),
        in_specs=[row(D_MODEL), _const_spec(n1.shape), _const_spec(wgu.shape), _const_spec(wd.shape),
                  _const_spec(nm.shape), _const_spec(win.shape), _const_spec(gsum.shape),
                  _const_spec(qkg.shape)],
        out_specs=[row(D_MODEL), row(CONV_CH), row(DIFF_WIDTH), row(DIFF_WIDTH), row(DIFF_WIDTH)],
        out_shape=[jax.ShapeDtypeStruct((t, D_MODEL), F32), jax.ShapeDtypeStruct((t, CONV_CH), F32),
                   jax.ShapeDtypeStruct((t, DIFF_WIDTH), BF16), jax.ShapeDtypeStruct((t, DIFF_WIDTH), BF16),
                   jax.ShapeDtypeStruct((t, DIFF_WIDTH), BF16)],
        scratch_shapes=[pltpu.VMEM((tm, D_FF), BF16)],
        compiler_params=pltpu.CompilerParams(dimension_semantics=("arbitrary",),
                                             vmem_limit_bytes=VMEM_LIMIT),
        name="ffn_proj",
    )(x, n1, wgu, wd, nm, win, gsum, qkg)


def _mix_ffn_body(seq_tiles, x_ref, ucur_ref, uprev_ref, att_ref, cw_ref, cb_ref, cn_ref, woc_ref,
                  wod_ref, n2_ref, wgu_ref, wd_ref, o_ref, ubuf, cbuf, act_ref):
    tm = x_ref.shape[0]
    first = (pl.program_id(0) % seq_tiles) == 0
    ubuf[0:CONV_HALO, :] = jnp.where(first, 0.0, uprev_ref[...])
    ubuf[CONV_HALO:, :] = ucur_ref[...]
    base = CONV_HALO - (CONV_K - 1)
    for r0 in range(0, tm, CONV_ROWS):
        acc = jnp.broadcast_to(cb_ref[...], (CONV_ROWS, CONV_CH))
        for j in range(CONV_K):
            acc = acc + cw_ref[j:j + 1, :] * ubuf[r0 + base + j:r0 + base + j + CONV_ROWS, :]
        c = _rms(acc, cn_ref[...])
        cbuf[r0:r0 + CONV_ROWS, :] = (c * jax.nn.sigmoid(c)).astype(BF16)
    mix = (jnp.dot(cbuf[...], woc_ref[...], preferred_element_type=F32)
           + jnp.dot(att_ref[...], wod_ref[...], preferred_element_type=F32))
    x1 = x_ref[...] + mix
    o_ref[...] = _ffn(x1, n2_ref, wgu_ref, wd_ref, act_ref)


def _mix_ffn(x, u, att, cw, cb, cn, woc, wod, n2, wgu, wd, seq):
    t = x.shape[0]
    tm = ROW_TILE
    row = lambda w: pl.BlockSpec((tm, w), lambda i: (i, 0))
    halo_per_tile = tm // CONV_HALO
    prev = pl.BlockSpec((CONV_HALO, CONV_CH), lambda i: (jnp.maximum(i * halo_per_tile - 1, 0), 0))
    return pl.pallas_call(
        functools.partial(_mix_ffn_body, seq // tm),
        grid=(t // tm,),
        in_specs=[row(D_MODEL), row(CONV_CH), prev, row(DIFF_WIDTH), _const_spec(cw.shape),
                  _const_spec(cb.shape), _const_spec(cn.shape), _const_spec(woc.shape),
                  _const_spec(wod.shape), _const_spec(n2.shape), _const_spec(wgu.shape),
                  _const_spec(wd.shape)],
        out_specs=row(D_MODEL),
        out_shape=jax.ShapeDtypeStruct((t, D_MODEL), F32),
        scratch_shapes=[pltpu.VMEM((tm + CONV_HALO, CONV_CH), F32), pltpu.VMEM((tm, CONV_CH), BF16),
                        pltpu.VMEM((tm, D_FF), BF16)],
        compiler_params=pltpu.CompilerParams(dimension_semantics=("arbitrary",),
                                             vmem_limit_bytes=VMEM_LIMIT),
        name="mix_ffn",
    )(x, u, u, att, cw, cb, cn, woc, wod, n2, wgu, wd)


def _attn_body(lam_init, lamv_ref, q_ref, k_ref, vt_ref, bp_ref, bd_ref, sg_ref, o_ref,
               qbd_ref, m_ref, l_ref, acc1_ref, acc2_ref):
    T = ATT_TILE
    i = pl.program_id(2)

    qt = q_ref[0].astype(F32).T
    row = lax.broadcasted_iota(jnp.int32, qt.shape, 0)
    qbd_ref[:, :T] = jnp.where(row < HEAD_DIM, qt, 0.0).astype(BF16)
    qbd_ref[:, T:] = jnp.where(row >= HEAD_DIM, qt, 0.0).astype(BF16)
    m_ref[...] = jnp.full(m_ref.shape, NEG_BIG, F32)
    l_ref[...] = jnp.zeros(l_ref.shape, F32)
    acc1_ref[...] = jnp.zeros(acc1_ref.shape, F32)
    acc2_ref[...] = jnp.zeros(acc2_ref.shape, F32)

    def block(j, bias_ref):
        kb = k_ref[0, pl.ds(pl.multiple_of(j * T, T), T), :]
        s = jnp.dot(kb, qbd_ref[...], preferred_element_type=F32)
        if bias_ref is not None:
            b = bias_ref[0]
            s = s + jnp.concatenate([b, b], axis=1)
        m_old = m_ref[...]
        m_new = jnp.maximum(m_old, jnp.max(s, axis=0, keepdims=True))
        alpha = jnp.exp2(m_old - m_new)
        p = jnp.exp2(s - m_new)
        l_ref[...] = alpha * l_ref[...] + jnp.sum(p, axis=0, keepdims=True)
        m_ref[...] = m_new
        pb = p.astype(BF16)
        vtb = vt_ref[0, 0, j]
        acc1_ref[...] = alpha[:, :T] * acc1_ref[...] + jnp.dot(vtb, pb[:, :T], preferred_element_type=F32)
        acc2_ref[...] = alpha[:, T:] * acc2_ref[...] + jnp.dot(vtb, pb[:, T:], preferred_element_type=F32)

    def far(j, carry):
        block(j, None)
        return carry

    lax.fori_loop(0, jnp.maximum(i - 1, 0), far, 0)

    @pl.when(i > 0)
    def _():
        block(i - 1, bp_ref)

    block(i, bd_ref)

    lv = lamv_ref[...]
    lam = (jnp.exp(jnp.sum(lv[0:1] * lv[1:2], axis=1, keepdims=True))
           - jnp.exp(jnp.sum(lv[2:3] * lv[3:4], axis=1, keepdims=True)) + lam_init)
    l = l_ref[...]
    d = acc1_ref[...] / l[:, :T] - lam * (acc2_ref[...] / l[:, T:])
    ms = jnp.mean(d * d, axis=0, keepdims=True)
    y = d * lax.rsqrt(ms + EPS)
    o_ref[0] = (y.T * sg_ref[...]).astype(o_ref.dtype)


def _diff_attn(q, k, vt, bp, bd, lamv, sg, lam_init):
    b, s, _ = q.shape
    T = ATT_TILE
    nq = s // T
    return pl.pallas_call(
        functools.partial(_attn_body, lam_init),
        grid=(b, N_HEADS, nq),
        in_specs=[
            pl.BlockSpec(lamv.shape, lambda bi, h, i: (0, 0)),
            pl.BlockSpec((1, T, 2 * HEAD_DIM), lambda bi, h, i: (bi, i, h)),
            pl.BlockSpec((1, s, 2 * HEAD_DIM), lambda bi, h, i: (bi, 0, h)),
            pl.BlockSpec((1, 1, nq, 2 * HEAD_DIM, T), lambda bi, h, i: (bi, h, 0, 0, 0)),
            pl.BlockSpec((1, T, T), lambda bi, h, i: (h, 0, 0)),
            pl.BlockSpec((1, T, T), lambda bi, h, i: (h, 0, 0)),
            pl.BlockSpec(sg.shape, lambda bi, h, i: (0, 0)),
        ],
        out_specs=pl.BlockSpec((1, T, 2 * HEAD_DIM), lambda bi, h, i: (bi, i, h)),
        out_shape=jax.ShapeDtypeStruct((b, s, DIFF_WIDTH), BF16),
        scratch_shapes=[pltpu.VMEM((2 * HEAD_DIM, 2 * T), BF16), pltpu.VMEM((1, 2 * T), F32),
                        pltpu.VMEM((1, 2 * T), F32), pltpu.VMEM((2 * HEAD_DIM, T), F32),
                        pltpu.VMEM((2 * HEAD_DIM, T), F32)],
        compiler_params=pltpu.CompilerParams(dimension_semantics=("arbitrary", "arbitrary", "arbitrary"),
                                             vmem_limit_bytes=VMEM_LIMIT),
        name="diff_attn",
    )(lamv, q, k, vt, bp, bd, sg)


def _rel_bucket_np(n):
    max_exact = NUM_BUCKETS // 2
    nf = np.maximum(n, 1).astype(np.float32)
    large = max_exact + (np.log(nf / max_exact) / math.log(MAX_DISTANCE / max_exact)
                         * (NUM_BUCKETS - max_exact)).astype(np.int32)
    large = np.minimum(large, NUM_BUCKETS - 1)
    return np.where(n < max_exact, n, large)


def _bias_tiles(rel_bias):
    T = ATT_TILE
    c = np.arange(T)[:, None]
    r = np.arange(T)[None, :]
    far = rel_bias[NUM_BUCKETS - 1]
    tab = (rel_bias - far[None, :]).astype(F32) * LOG2E
    dist_d = r - c
    bd = jnp.where((dist_d >= 0)[:, :, None], tab[_rel_bucket_np(np.maximum(dist_d, 0))], NEG_BIG)
    bp = tab[_rel_bucket_np(r - c + T)]
    return bp.transpose(2, 0, 1), bd.transpose(2, 0, 1)


def kernel(x, rel_bias, ffn1_norm, ffn1_w_gate, ffn1_w_up, ffn1_w_down, mix_norm, w_in, conv_w, conv_b,
           conv_norm, q_norm, k_norm, lambda_q1, lambda_k1, lambda_q2, lambda_k2, subln_norm, w_out,
           ffn2_norm, ffn2_w_gate, ffn2_w_up, ffn2_w_down):
    B, S, _ = x.shape
    depth = w_in.shape[0]
    assert S % ATT_TILE == 0 and S % ROW_TILE == 0 and ATT_TILE >= MAX_DISTANCE
    nk = S // ATT_TILE

    def pack_gu(wg, wu):
        g3 = wg.reshape(D_MODEL, N_FF_CHUNKS, FF_CHUNK).transpose(1, 0, 2)
        u3 = wu.reshape(D_MODEL, N_FF_CHUNKS, FF_CHUNK).transpose(1, 0, 2)
        return jnp.concatenate([g3, u3], axis=-1).astype(BF16)

    grp = np.arange(GROUP_MM) // HEAD_DIM
    gsum = jnp.asarray(grp[:, None] == grp[None, :], BF16)
    bp, bd = _bias_tiles(rel_bias.astype(F32))
    row2 = lambda a: a.reshape(1, -1).astype(F32)

    xf = x.reshape(B * S, D_MODEL)
    for l in range(depth):
        lam_init = 0.8 - 0.6 * math.exp(-0.3 * l)
        qkg = jnp.concatenate([jnp.tile(q_norm[l].astype(F32), 2 * N_HEADS) * (LOG2E / math.sqrt(HEAD_DIM)),
                               jnp.tile(k_norm[l].astype(F32), 2 * N_HEADS)]).reshape(1, -1)
        x1, u, q, k, v = _ffn_proj(xf, row2(ffn1_norm[l]), pack_gu(ffn1_w_gate[l], ffn1_w_up[l]),
                                   ffn1_w_down[l].astype(BF16), row2(mix_norm[l]), w_in[l].astype(BF16),
                                   gsum, qkg)
        vt = v.reshape(B, nk, ATT_TILE, N_HEADS, 2 * HEAD_DIM).transpose(0, 3, 1, 4, 2)
        lamv = jnp.stack([lambda_q1[l], lambda_k1[l], lambda_q2[l], lambda_k2[l]]).astype(F32)
        sg = row2(subln_norm[l]) * (1.0 - lam_init)
        att = _diff_attn(q.reshape(B, S, DIFF_WIDTH), k.reshape(B, S, DIFF_WIDTH), vt, bp, bd, lamv, sg,
                         lam_init)
        xf = _mix_ffn(x1, u, att.reshape(B * S, DIFF_WIDTH), conv_w[l].astype(F32), row2(conv_b[l]),
                      row2(conv_norm[l]), w_out[l, :CONV_CH].astype(BF16), w_out[l, CONV_CH:].astype(BF16),
                      row2(ffn2_norm[l]), pack_gu(ffn2_w_gate[l], ffn2_w_up[l]),
                      ffn2_w_down[l].astype(BF16), S)
    return xf.reshape(B, S, D_MODEL)
```

```python
import functools
import math

import jax
import jax.numpy as jnp
import numpy as np
from jax import lax
from jax.experimental import pallas as pl
from jax.experimental.pallas import tpu as pltpu

F32 = jnp.float32
BF16 = jnp.bfloat16

D_MODEL = 1024
CONV_CH = 512
CONV_K = 31
DIFF_WIDTH = 512
HEAD_DIM = 64
N_HEADS = 4
D_FF = 2816
NUM_BUCKETS = 32
MAX_DISTANCE = 128
EPS = 1e-6
LOG2E = 1.4426950408889634
NEG_BIG = -1e30
SAFE_EXP2_BOUND = 90.0

FF_CHUNK = 256
N_FF_CHUNKS = D_FF // FF_CHUNK
ROW_TILE = 512
ATT_TILE = 512
FAR_STEP = 4
CONV_HALO = 32
CONV_ROWS = 64
GROUP_MM = 256
VMEM_LIMIT = 56 * 1024 * 1024


def _rms(x, g):
    ms = jnp.mean(x * x, axis=-1, keepdims=True)
    return x * lax.rsqrt(ms + EPS) * g


def _ffn(x, g_ref, wgu_ref, wd_ref, act_ref):
    h = _rms(x, g_ref[...]).astype(BF16)
    for c in range(N_FF_CHUNKS):
        gu = jnp.dot(h, wgu_ref[c], preferred_element_type=F32)
        gate = gu[:, :FF_CHUNK]
        up = gu[:, FF_CHUNK:]
        act_ref[:, c * FF_CHUNK:(c + 1) * FF_CHUNK] = (gate * jax.nn.sigmoid(gate) * up).astype(BF16)
    y = jnp.dot(act_ref[...], wd_ref[...], preferred_element_type=F32)
    return x + 0.5 * y


def _ffn_proj_body(x_ref, n1_ref, wgu_ref, wd_ref, nm_ref, win_ref, gsum_ref, qkg_ref,
                   xo_ref, u_ref, q_ref, k_ref, v_ref, act_ref):
    x1 = _ffn(x_ref[...], n1_ref, wgu_ref, wd_ref, act_ref)
    xo_ref[...] = x1
    h = _rms(x1, nm_ref[...]).astype(BF16)
    ag = jnp.dot(h, win_ref[:, 0:2 * CONV_CH], preferred_element_type=F32)
    u_ref[...] = ag[:, :CONV_CH] * jax.nn.sigmoid(ag[:, CONV_CH:])
    qk = jnp.dot(h, win_ref[:, 2 * CONV_CH:2 * CONV_CH + 2 * DIFF_WIDTH], preferred_element_type=F32)
    sq = (qk * qk).astype(BF16)
    ss = jnp.concatenate(
        [jnp.dot(sq[:, c * GROUP_MM:(c + 1) * GROUP_MM], gsum_ref[...], preferred_element_type=F32)
         for c in range(2 * DIFF_WIDTH // GROUP_MM)], axis=1)
    qkn = qk * lax.rsqrt(ss * (1.0 / HEAD_DIM) + EPS) * qkg_ref[...]
    q_ref[...] = qkn[:, :DIFF_WIDTH].astype(BF16)
    k_ref[...] = qkn[:, DIFF_WIDTH:].astype(BF16)
    v = jnp.dot(h, win_ref[:, 2 * CONV_CH + 2 * DIFF_WIDTH:], preferred_element_type=F32)
    v_ref[...] = v.astype(BF16)


def _const_spec(shape):
    nd = len(shape)
    return pl.BlockSpec(shape, lambda i: (0,) * nd, pipeline_mode=pl.Buffered(1))


def _ffn_proj(x, n1, wgu, wd, nm, win, gsum, qkg):
    t = x.shape[0]
    tm = ROW_TILE
    row = lambda w: pl.BlockSpec((tm, w), lambda i: (i, 0))
    return pl.pallas_call(
        _ffn_proj_body,
        grid=(t // tm,),
        in_specs=[row(D_MODEL), _const_spec(n1.shape), _const_spec(wgu.shape), _const_spec(wd.shape),
                  _const_spec(nm.shape), _const_spec(win.shape), _const_spec(gsum.shape),
                  _const_spec(qkg.shape)],
        out_specs=[row(D_MODEL), row(CONV_CH), row(DIFF_WIDTH), row(DIFF_WIDTH), row(DIFF_WIDTH)],
        out_shape=[jax.ShapeDtypeStruct((t, D_MODEL), F32), jax.ShapeDtypeStruct((t, CONV_CH), F32),
                   jax.ShapeDtypeStruct((t, DIFF_WIDTH), BF16), jax.ShapeDtypeStruct((t, DIFF_WIDTH), BF16),
                   jax.ShapeDtypeStruct((t, DIFF_WIDTH), BF16)],
        scratch_shapes=[pltpu.VMEM((tm, D_FF), BF16)],
        compiler_params=pltpu.CompilerParams(dimension_semantics=("arbitrary",),
                                             vmem_limit_bytes=VMEM_LIMIT),
        name="ffn_proj",
    )(x, n1, wgu, wd, nm, win, gsum, qkg)


def _mix_ffn_body(seq_tiles, x_ref, ucur_ref, uprev_ref, att_ref, cw_ref, cb_ref, cn_ref, woc_ref,
                  wod_ref, n2_ref, wgu_ref, wd_ref, o_ref, ubuf, cbuf, act_ref):
    tm = x_ref.shape[0]
    first = (pl.program_id(0) % seq_tiles) == 0
    ubuf[0:CONV_HALO, :] = jnp.where(first, 0.0, uprev_ref[...])
    ubuf[CONV_HALO:, :] = ucur_ref[...]
    base = CONV_HALO - (CONV_K - 1)
    for r0 in range(0, tm, CONV_ROWS):
        acc = jnp.broadcast_to(cb_ref[...], (CONV_ROWS, CONV_CH))
        for j in range(CONV_K):
            acc = acc + cw_ref[j:j + 1, :] * ubuf[r0 + base + j:r0 + base + j + CONV_ROWS, :]
        c = _rms(acc, cn_ref[...])
        cbuf[r0:r0 + CONV_ROWS, :] = (c * jax.nn.sigmoid(c)).astype(BF16)
    mix = (jnp.dot(cbuf[...], woc_ref[...], preferred_element_type=F32)
           + jnp.dot(att_ref[...], wod_ref[...], preferred_element_type=F32))
    x1 = x_ref[...] + mix
    o_ref[...] = _ffn(x1, n2_ref, wgu_ref, wd_ref, act_ref)


def _mix_ffn(x, u, att, cw, cb, cn, woc, wod, n2, wgu, wd, seq):
    t = x.shape[0]
    tm = ROW_TILE
    row = lambda w: pl.BlockSpec((tm, w), lambda i: (i, 0))
    halo_per_tile = tm // CONV_HALO
    prev = pl.BlockSpec((CONV_HALO, CONV_CH), lambda i: (jnp.maximum(i * halo_per_tile - 1, 0), 0))
    return pl.pallas_call(
        functools.partial(_mix_ffn_body, seq // tm),
        grid=(t // tm,),
        in_specs=[row(D_MODEL), row(CONV_CH), prev, row(DIFF_WIDTH), _const_spec(cw.shape),
                  _const_spec(cb.shape), _const_spec(cn.shape), _const_spec(woc.shape),
                  _const_spec(wod.shape), _const_spec(n2.shape), _const_spec(wgu.shape),
                  _const_spec(wd.shape)],
        out_specs=row(D_MODEL),
        out_shape=jax.ShapeDtypeStruct((t, D_MODEL), F32),
        scratch_shapes=[pltpu.VMEM((tm + CONV_HALO, CONV_CH), F32), pltpu.VMEM((tm, CONV_CH), BF16),
                        pltpu.VMEM((tm, D_FF), BF16)],
        compiler_params=pltpu.CompilerParams(dimension_semantics=("arbitrary",),
                                             vmem_limit_bytes=VMEM_LIMIT),
        name="mix_ffn",
    )(x, u, u, att, cw, cb, cn, woc, wod, n2, wgu, wd)


def _attn_body(lam_init, online, lamv_ref, q_ref, k_ref, vt_ref, bp_ref, bd_ref, sg_ref, o_ref,
               qbd_ref, m_ref, l_ref, acc1_ref, acc2_ref):
    T = ATT_TILE
    i = pl.program_id(2)

    qt = q_ref[0].astype(F32).T
    row = lax.broadcasted_iota(jnp.int32, qt.shape, 0)
    qbd_ref[:, :T] = jnp.where(row < HEAD_DIM, qt, 0.0).astype(BF16)
    qbd_ref[:, T:] = jnp.where(row >= HEAD_DIM, qt, 0.0).astype(BF16)
    m_ref[...] = jnp.full(m_ref.shape, NEG_BIG, F32)
    l_ref[...] = jnp.zeros(l_ref.shape, F32)
    acc1_ref[...] = jnp.zeros(acc1_ref.shape, F32)
    acc2_ref[...] = jnp.zeros(acc2_ref.shape, F32)

    def block(j, nblk, bias_refs=()):
        kb = k_ref[0, pl.ds(pl.multiple_of(j * T, T), nblk * T), :]
        s = jnp.dot(kb, qbd_ref[...], preferred_element_type=F32)
        if bias_refs:
            b = jnp.concatenate([r[0] for r in bias_refs], axis=0)
            s = s + jnp.concatenate([b, b], axis=1)
        vtb = jnp.concatenate([vt_ref[0, 0, j + t] for t in range(nblk)], axis=1)
        if not online:
            p = jnp.exp2(s)
            l_ref[...] += jnp.sum(p, axis=0, keepdims=True)
            pb = p.astype(BF16)
            acc1_ref[...] += jnp.dot(vtb, pb[:, :T], preferred_element_type=F32)
            acc2_ref[...] += jnp.dot(vtb, pb[:, T:], preferred_element_type=F32)
            return
        m_old = m_ref[...]
        m_new = jnp.maximum(m_old, jnp.max(s, axis=0, keepdims=True))
        alpha = jnp.exp2(m_old - m_new)
        p = jnp.exp2(s - m_new)
        l_ref[...] = alpha * l_ref[...] + jnp.sum(p, axis=0, keepdims=True)
        m_ref[...] = m_new
        pb = p.astype(BF16)
        acc1_ref[...] = alpha[:, :T] * acc1_ref[...] + jnp.dot(vtb, pb[:, :T], preferred_element_type=F32)
        acc2_ref[...] = alpha[:, T:] * acc2_ref[...] + jnp.dot(vtb, pb[:, T:], preferred_element_type=F32)

    n_far = jnp.maximum(i - 1, 0)

    def far_step(jj, carry):
        block(FAR_STEP * jj, FAR_STEP)
        return carry

    lax.fori_loop(0, n_far // FAR_STEP, far_step, 0)
    done = (n_far // FAR_STEP) * FAR_STEP
    size = FAR_STEP // 2
    while size >= 1:
        take = (n_far - done) >= size

        @pl.when(take)
        def _(done=done, size=size):
            block(done, size)

        done = done + jnp.where(take, size, 0)
        size //= 2

    @pl.when(i > 0)
    def _():
        block(i - 1, 2, (bp_ref, bd_ref))

    @pl.when(i == 0)
    def _():
        block(i, 1, (bd_ref,))

    lv = lamv_ref[...]
    lam = (jnp.exp(jnp.sum(lv[0:1] * lv[1:2], axis=1, keepdims=True))
           - jnp.exp(jnp.sum(lv[2:3] * lv[3:4], axis=1, keepdims=True)) + lam_init)
    l = l_ref[...]
    d = acc1_ref[...] / l[:, :T] - lam * (acc2_ref[...] / l[:, T:])
    ms = jnp.mean(d * d, axis=0, keepdims=True)
    y = d * lax.rsqrt(ms + EPS)
    o_ref[0] = (y.T * sg_ref[...]).astype(o_ref.dtype)


def _diff_attn(online, lam_init, q, k, vt, bp, bd, lamv, sg):
    b, s, _ = q.shape
    T = ATT_TILE
    nq = s // T
    return pl.pallas_call(
        functools.partial(_attn_body, lam_init, online),
        grid=(b, N_HEADS, nq),
        in_specs=[
            pl.BlockSpec(lamv.shape, lambda bi, h, i: (0, 0)),
            pl.BlockSpec((1, T, 2 * HEAD_DIM), lambda bi, h, i: (bi, i, h)),
            pl.BlockSpec((1, s, 2 * HEAD_DIM), lambda bi, h, i: (bi, 0, h)),
            pl.BlockSpec((1, 1, nq, 2 * HEAD_DIM, T), lambda bi, h, i: (bi, h, 0, 0, 0)),
            pl.BlockSpec((1, T, T), lambda bi, h, i: (h, 0, 0)),
            pl.BlockSpec((1, T, T), lambda bi, h, i: (h, 0, 0)),
            pl.BlockSpec(sg.shape, lambda bi, h, i: (0, 0)),
        ],
        out_specs=pl.BlockSpec((1, T, 2 * HEAD_DIM), lambda bi, h, i: (bi, i, h)),
        out_shape=jax.ShapeDtypeStruct((b, s, DIFF_WIDTH), BF16),
        scratch_shapes=[pltpu.VMEM((2 * HEAD_DIM, 2 * T), BF16), pltpu.VMEM((1, 2 * T), F32),
                        pltpu.VMEM((1, 2 * T), F32), pltpu.VMEM((2 * HEAD_DIM, T), F32),
                        pltpu.VMEM((2 * HEAD_DIM, T), F32)],
        compiler_params=pltpu.CompilerParams(dimension_semantics=("arbitrary", "arbitrary", "arbitrary"),
                                             vmem_limit_bytes=VMEM_LIMIT),
        name="diff_attn_online" if online else "diff_attn",
    )(lamv, q, k, vt, bp, bd, sg)


def _rel_bucket_np(n):
    max_exact = NUM_BUCKETS // 2
    nf = np.maximum(n, 1).astype(np.float32)
    large = max_exact + (np.log(nf / max_exact) / math.log(MAX_DISTANCE / max_exact)
                         * (NUM_BUCKETS - max_exact)).astype(np.int32)
    large = np.minimum(large, NUM_BUCKETS - 1)
    return np.where(n < max_exact, n, large)


def _bias_body(tab_ref, idx_ref, o_ref):
    h = pl.program_id(1)
    idx = idx_ref[0]
    acc = jnp.zeros(idx.shape, F32)
    for b in range(NUM_BUCKETS):
        acc = jnp.where(idx == b, tab_ref[b, h], acc)
    o_ref[0, 0] = jnp.where(idx < 0, NEG_BIG, acc)


def _bias_tiles(tab):
    T = ATT_TILE
    c = np.arange(T)[:, None]
    r = np.arange(T)[None, :]
    idx_d = np.where(r - c >= 0, _rel_bucket_np(np.maximum(r - c, 0)), -1)
    idx = jnp.asarray(np.stack([_rel_bucket_np(r - c + T), idx_d]).astype(np.int32))
    return pl.pallas_call(
        _bias_body,
        grid=(2, N_HEADS),
        in_specs=[pl.BlockSpec(memory_space=pltpu.SMEM),
                  pl.BlockSpec((1, T, T), lambda t, h: (t, 0, 0))],
        out_specs=pl.BlockSpec((1, 1, T, T), lambda t, h: (t, h, 0, 0)),
        out_shape=jax.ShapeDtypeStruct((2, N_HEADS, T, T), F32),
        name="bias_tiles",
    )(tab, idx)


def kernel(x, rel_bias, ffn1_norm, ffn1_w_gate, ffn1_w_up, ffn1_w_down, mix_norm, w_in, conv_w, conv_b,
           conv_norm, q_norm, k_norm, lambda_q1, lambda_k1, lambda_q2, lambda_k2, subln_norm, w_out,
           ffn2_norm, ffn2_w_gate, ffn2_w_up, ffn2_w_down):
    B, S, _ = x.shape
    depth = w_in.shape[0]
    assert S % ATT_TILE == 0 and S % ROW_TILE == 0 and ATT_TILE >= MAX_DISTANCE
    nk = S // ATT_TILE

    def pack_gu(wg, wu):
        g3 = wg.reshape(D_MODEL, N_FF_CHUNKS, FF_CHUNK).transpose(1, 0, 2)
        u3 = wu.reshape(D_MODEL, N_FF_CHUNKS, FF_CHUNK).transpose(1, 0, 2)
        return jnp.concatenate([g3, u3], axis=-1).astype(BF16)

    grp = np.arange(GROUP_MM) // HEAD_DIM
    gsum = jnp.asarray(grp[:, None] == grp[None, :], BF16)
    tab = (rel_bias - rel_bias[NUM_BUCKETS - 1:]).astype(F32) * LOG2E
    bias = _bias_tiles(tab)
    bp, bd = bias[0], bias[1]
    row2 = lambda a: a.reshape(1, -1).astype(F32)

    xf = x.reshape(B * S, D_MODEL)
    for l in range(depth):
        lam_init = 0.8 - 0.6 * math.exp(-0.3 * l)
        qg = q_norm[l].astype(F32) * (LOG2E / math.sqrt(HEAD_DIM))
        kg = k_norm[l].astype(F32)
        qkg = jnp.concatenate([jnp.tile(qg, 2 * N_HEADS), jnp.tile(kg, 2 * N_HEADS)]).reshape(1, -1)
        score_bound = (HEAD_DIM * 1.02) * jnp.max(jnp.abs(qg)) * jnp.max(jnp.abs(kg)) + jnp.max(jnp.abs(tab))
        x1, u, q, k, v = _ffn_proj(xf, row2(ffn1_norm[l]), pack_gu(ffn1_w_gate[l], ffn1_w_up[l]),
                                   ffn1_w_down[l].astype(BF16), row2(mix_norm[l]), w_in[l].astype(BF16),
                                   gsum, qkg)
        vt = v.reshape(B, nk, ATT_TILE, N_HEADS, 2 * HEAD_DIM).transpose(0, 3, 1, 4, 2)
        lamv = jnp.stack([lambda_q1[l], lambda_k1[l], lambda_q2[l], lambda_k2[l]]).astype(F32)
        sg = row2(subln_norm[l]) * (1.0 - lam_init)
        att = lax.cond(score_bound <= SAFE_EXP2_BOUND,
                       functools.partial(_diff_attn, False, lam_init),
                       functools.partial(_diff_attn, True, lam_init),
                       q.reshape(B, S, DIFF_WIDTH), k.reshape(B, S, DIFF_WIDTH), vt, bp, bd, lamv, sg)
        xf = _mix_ffn(x1, u, att.reshape(B * S, DIFF_WIDTH), conv_w[l].astype(F32), row2(conv_b[l]),
                      row2(conv_norm[l]), w_out[l, :CONV_CH].astype(BF16), w_out[l, CONV_CH:].astype(BF16),
                      row2(ffn2_norm[l]), pack_gu(ffn2_w_gate[l], ffn2_w_up[l]),
                      ffn2_w_down[l].astype(BF16), S)
    return xf.reshape(B, S, D_MODEL)
```

```python
import functools
import math

import jax
import jax.numpy as jnp
import numpy as np
from jax import lax
from jax.experimental import pallas as pl
from jax.experimental.pallas import tpu as pltpu

F32 = jnp.float32
BF16 = jnp.bfloat16

D_MODEL = 1024
CONV_CH = 512
CONV_K = 31
DIFF_WIDTH = 512
HEAD_DIM = 64
N_HEADS = 4
D_FF = 2816
NUM_BUCKETS = 32
MAX_DISTANCE = 128
EPS = 1e-6
LOG2E = 1.4426950408889634
NEG_BIG = -1e30
SAFE_EXP2_BOUND = 90.0

FF_CHUNK = 256
N_FF_CHUNKS = D_FF // FF_CHUNK
ROW_TILE = 512
ATT_TILE = 512
FAR_STEP = 4
SIDE_WORK_LAG = 2
SUBLANES = 8
CONV_HALO = 32
CONV_ROWS = 32
GROUP_MM = 256
VMEM_LIMIT = 56 * 1024 * 1024


def _rms(x, g):
    ms = jnp.mean(x * x, axis=-1, keepdims=True)
    return x * lax.rsqrt(ms + EPS) * g


def _zero_after(v):
    bits = pltpu.bitcast(v[0:1, :], jnp.uint32)
    return pltpu.bitcast((bits >> 16) >> 16, F32)


def _ffn(x, g_ref, wgu_ref, wd_ref, act_ref, side_work=None):
    h = _rms(x, g_ref[...]).astype(BF16)
    zs = [None] * SIDE_WORK_LAG
    for c in range(N_FF_CHUNKS):
        lo = c * FF_CHUNK
        gate = jnp.dot(h, wgu_ref[:, lo:lo + FF_CHUNK], preferred_element_type=F32)
        up = jnp.dot(h, wgu_ref[:, D_FF + lo:D_FF + lo + FF_CHUNK], preferred_element_type=F32)
        z = zs.pop(0)
        if z is not None:
            up = up + z
        act_ref[:, c * FF_CHUNK:(c + 1) * FF_CHUNK] = (gate * jax.nn.sigmoid(gate) * up).astype(BF16)
        zs.append(side_work(c, gate) if side_work is not None else None)
    y = jnp.dot(act_ref[...], wd_ref[...], preferred_element_type=F32)
    return x + 0.5 * y


def _conv_prev_tile(ubuf, ush, cw_ref, cb_ref, cn_ref, c_ref):
    tm = c_ref.shape[0]
    base = CONV_HALO - (CONV_K - 1)
    n_pieces = N_FF_CHUNKS
    rows_per = -(-tm // (n_pieces * SUBLANES)) * SUBLANES

    def piece(p, gate):
        r0 = p * rows_per
        nr = min(rows_per, tm - r0)
        if nr <= 0:
            return None
        groups = nr // SUBLANES
        z = _zero_after(gate)
        bias = cb_ref[...] + jnp.concatenate([z] * (CONV_CH // FF_CHUNK), axis=1)
        acc = jnp.broadcast_to(bias, (groups, SUBLANES, CONV_CH))
        for j in range(CONV_K):
            a, b = divmod(base + j, SUBLANES)
            lo = r0 + a * SUBLANES
            rows = ubuf[lo:lo + nr, :] if b == 0 else ush[b - 1, lo:lo + nr, :]
            acc = acc + cw_ref[j][None] * rows.reshape(groups, SUBLANES, CONV_CH)
        c = _rms(acc.reshape(nr, CONV_CH), cn_ref[...])
        c = c * jax.nn.sigmoid(c)
        c_ref[r0:r0 + nr, :] = c.astype(c_ref.dtype)
        return _zero_after(jnp.sum(c, axis=0, keepdims=True))[:, :FF_CHUNK]

    return piece


def _ffn_proj_body(seq_tiles, x_ref, n1_ref, wgu_ref, wd_ref, nm_ref, win_ref, gsum_ref, qkg_ref,
                   cw_ref, cb_ref, cn_ref, xo_ref, c_ref, q_ref, k_ref, vt_ref, act_ref, ubuf, ush):
    i = pl.program_id(0)
    tm = x_ref.shape[0]

    @pl.when(i == 0)
    def _():
        ubuf[...] = jnp.zeros(ubuf.shape, F32)
        ush[...] = jnp.zeros(ush.shape, F32)

    conv_piece = _conv_prev_tile(ubuf, ush, cw_ref, cb_ref, cn_ref, c_ref)
    x1 = _ffn(x_ref[...], n1_ref, wgu_ref, wd_ref, act_ref, side_work=conv_piece)
    xo_ref[...] = x1
    h = _rms(x1, nm_ref[...]).astype(BF16)
    ag = jnp.dot(h, win_ref[:, 0:2 * CONV_CH], preferred_element_type=F32)
    u = ag[:, :CONV_CH] * jax.nn.sigmoid(ag[:, CONV_CH:])
    qk =jnp.dot(h, win_ref[:, 2 * CONV_CH:2 * CONV_CH + 2 * DIFF_WIDTH], preferred_element_type=F32)
    sq = (qk * qk).astype(BF16)
    ss = jnp.concatenate(
        [jnp.dot(sq[:, c * GROUP_MM:(c + 1) * GROUP_MM], gsum_ref[...], preferred_element_type=F32)
         for c in range(2 * DIFF_WIDTH // GROUP_MM)], axis=1)
    qkn = qk * lax.rsqrt(ss * (1.0 / HEAD_DIM) + EPS) * qkg_ref[...]
    q_ref[...] = qkn[:, :DIFF_WIDTH].astype(BF16)
    k_ref[...] = qkn[:, DIFF_WIDTH:].astype(BF16)
    v = jnp.dot(h, win_ref[:, 2 * CONV_CH + 2 * DIFF_WIDTH:], preferred_element_type=F32)
    vt_ref[0, :, 0] = v.T.reshape(N_HEADS, 2 * HEAD_DIM, tm).astype(BF16)

    tail = ubuf[tm:tm + CONV_HALO, :]
    ubuf[0:CONV_HALO, :] = jnp.where(i % seq_tiles == 0, 0.0, tail)
    ubuf[CONV_HALO:, :] = u
    for b in range(1, SUBLANES):
        ush[b - 1] = ubuf[b:b + ush.shape[1], :]


def _const_spec(shape):
    nd = len(shape)
    return pl.BlockSpec(shape, lambda i: (0,) * nd, pipeline_mode=pl.Buffered(1))


def _ffn_proj(x, n1, wgu, wd, nm, win, gsum, qkg, cw, cb, cn, seq):
    t = x.shape[0]
    tm = ROW_TILE
    n = t // tm
    cur = lambda w: pl.BlockSpec((tm, w), lambda i: (jnp.minimum(i, n - 1), 0))
    prev = lambda w: pl.BlockSpec((tm, w), lambda i: (jnp.maximum(i - 1, 0), 0))
    consts = (n1, wgu, wd, nm, win, gsum, qkg, cw, cb, cn)
    seq_tiles = seq // tm

    def vt_index(i):
        ii = jnp.minimum(i, n - 1)
        return (ii // seq_tiles, 0, ii % seq_tiles, 0, 0)

    return pl.pallas_call(
        functools.partial(_ffn_proj_body, seq_tiles),
        grid=(n + 1,),
        in_specs=[cur(D_MODEL)] + [_const_spec(a.shape) for a in consts],
        out_specs=[cur(D_MODEL), prev(CONV_CH), cur(DIFF_WIDTH), cur(DIFF_WIDTH),
                   pl.BlockSpec((1, N_HEADS, 1, 2 * HEAD_DIM, tm), vt_index)],
        out_shape=[jax.ShapeDtypeStruct((t, D_MODEL), F32), jax.ShapeDtypeStruct((t, CONV_CH), BF16),
                   jax.ShapeDtypeStruct((t, DIFF_WIDTH), BF16), jax.ShapeDtypeStruct((t, DIFF_WIDTH), BF16),
                   jax.ShapeDtypeStruct((t // seq, N_HEADS, seq_tiles, 2 * HEAD_DIM, tm), BF16)],
        scratch_shapes=[pltpu.VMEM((tm, D_FF), BF16), pltpu.VMEM((tm + CONV_HALO, CONV_CH), F32),
                        pltpu.VMEM((SUBLANES - 1, tm + CONV_HALO - SUBLANES, CONV_CH), F32)],
        compiler_params=pltpu.CompilerParams(dimension_semantics=("arbitrary",),
                                             vmem_limit_bytes=VMEM_LIMIT),
        name="ffn_proj",
    )(x, *consts)


def _mix_ffn_body(x_ref, c_ref, att_ref, woc_ref, wod_ref, n2_ref, wgu_ref, wd_ref, o_ref, act_ref):
    mix = (jnp.dot(c_ref[...], woc_ref[...], preferred_element_type=F32)
           + jnp.dot(att_ref[...], wod_ref[...], preferred_element_type=F32))
    x1 = x_ref[...] + mix
    o_ref[...] = _ffn(x1, n2_ref, wgu_ref, wd_ref, act_ref)


def _mix_ffn(x, c, att, woc, wod, n2, wgu, wd):
    t = x.shape[0]
    tm = ROW_TILE
    row = lambda w: pl.BlockSpec((tm, w), lambda i: (i, 0))
    consts = (woc, wod, n2, wgu, wd)
    return pl.pallas_call(
        _mix_ffn_body,
        grid=(t // tm,),
        in_specs=[row(D_MODEL), row(CONV_CH), row(DIFF_WIDTH)] + [_const_spec(a.shape) for a in consts],
        out_specs=row(D_MODEL),
        out_shape=jax.ShapeDtypeStruct((t, D_MODEL), F32),
        scratch_shapes=[pltpu.VMEM((tm, D_FF), BF16)],
        compiler_params=pltpu.CompilerParams(dimension_semantics=("arbitrary",),
                                             vmem_limit_bytes=VMEM_LIMIT),
        name="mix_ffn",
    )(x, c, att, *consts)


def _attn_body(lam_init, online, lamv_ref, q_ref, k_ref, vt_ref, bp_ref, bd_ref, sg_ref, o_ref,
               qbd_ref, m_ref, l_ref, acc1_ref, acc2_ref):
    T = ATT_TILE
    i = pl.program_id(2)

    qt = q_ref[0].astype(F32).T
    row = lax.broadcasted_iota(jnp.int32, qt.shape, 0)
    qbd_ref[:, :T] = jnp.where(row < HEAD_DIM, qt, 0.0).astype(BF16)
    qbd_ref[:, T:] = jnp.where(row >= HEAD_DIM, qt, 0.0).astype(BF16)
    m_ref[...] = jnp.full(m_ref.shape, NEG_BIG, F32)
    l_ref[...] = jnp.zeros(l_ref.shape, F32)
    acc1_ref[...] = jnp.zeros(acc1_ref.shape, F32)
    acc2_ref[...] = jnp.zeros(acc2_ref.shape, F32)

    def block(j, nblk, bias_refs=()):
        kb = k_ref[0, pl.ds(pl.multiple_of(j * T, T), nblk * T), :]
        s = jnp.dot(kb, qbd_ref[...], preferred_element_type=F32)
        if bias_refs:
            b = jnp.concatenate([r[0] for r in bias_refs], axis=0)
            s = s + jnp.concatenate([b, b], axis=1)
        vtb = jnp.concatenate([vt_ref[0, 0, j + t] for t in range(nblk)], axis=1)
        if not online:
            p = jnp.exp2(s)
            l_ref[...] += jnp.sum(p, axis=0, keepdims=True)
            pb = p.astype(BF16)
            acc1_ref[...] += jnp.dot(vtb, pb[:, :T], preferred_element_type=F32)
            acc2_ref[...] += jnp.dot(vtb, pb[:, T:], preferred_element_type=F32)
            return
        m_old = m_ref[...]
        m_new = jnp.maximum(m_old, jnp.max(s, axis=0, keepdims=True))
        alpha = jnp.exp2(m_old - m_new)
        p = jnp.exp2(s - m_new)
        l_ref[...] = alpha * l_ref[...] + jnp.sum(p, axis=0, keepdims=True)
        m_ref[...] = m_new
        pb = p.astype(BF16)
        acc1_ref[...] = alpha[:, :T] * acc1_ref[...] + jnp.dot(vtb, pb[:, :T], preferred_element_type=F32)
        acc2_ref[...] = alpha[:, T:] * acc2_ref[...] + jnp.dot(vtb, pb[:, T:], preferred_element_type=F32)

    n_far = jnp.maximum(i - 1, 0)

    def far_step(jj, carry):
        block(FAR_STEP * jj, FAR_STEP)
        return carry

    lax.fori_loop(0, n_far // FAR_STEP, far_step, 0)
    done = (n_far // FAR_STEP) * FAR_STEP
    size = FAR_STEP // 2
    while size >= 1:
        take = (n_far - done) >= size

        @pl.when(take)
        def _(done=done, size=size):
            block(done, size)

        done = done + jnp.where(take, size, 0)
        size //= 2

    @pl.when(i > 0)
    def _():
        block(i - 1, 2, (bp_ref, bd_ref))

    @pl.when(i == 0)
    def _():
        block(i, 1, (bd_ref,))

    lv = lamv_ref[...]
    lam = (jnp.exp(jnp.sum(lv[0:1] * lv[1:2], axis=1, keepdims=True))
           - jnp.exp(jnp.sum(lv[2:3] * lv[3:4], axis=1, keepdims=True)) + lam_init)
    l = l_ref[...]
    d = acc1_ref[...] / l[:, :T] - lam * (acc2_ref[...] / l[:, T:])
    ms = jnp.mean(d * d, axis=0, keepdims=True)
    y = d * lax.rsqrt(ms + EPS)
    o_ref[0] = (y.T * sg_ref[...]).astype(o_ref.dtype)


def _diff_attn(online, lam_init, q, k, vt, bp, bd, lamv, sg):
    b, s, _ = q.shape
    T = ATT_TILE
    nq = s // T
    return pl.pallas_call(
        functools.partial(_attn_body, lam_init, online),
        grid=(b, N_HEADS, nq),
        in_specs=[
            pl.BlockSpec(lamv.shape, lambda bi, h, i: (0, 0)),
            pl.BlockSpec((1, T, 2 * HEAD_DIM), lambda bi, h, i: (bi, i, h)),
            pl.BlockSpec((1, s, 2 * HEAD_DIM), lambda bi, h, i: (bi, 0, h)),
            pl.BlockSpec((1, 1, nq, 2 * HEAD_DIM, T), lambda bi, h, i: (bi, h, 0, 0, 0)),
            pl.BlockSpec((1, T, T), lambda bi, h, i: (h, 0, 0)),
            pl.BlockSpec((1, T, T), lambda bi, h, i: (h, 0, 0)),
            pl.BlockSpec(sg.shape, lambda bi, h, i: (0, 0)),
        ],
        out_specs=pl.BlockSpec((1, T, 2 * HEAD_DIM), lambda bi, h, i: (bi, i, h)),
        out_shape=jax.ShapeDtypeStruct((b, s, DIFF_WIDTH), BF16),
        scratch_shapes=[pltpu.VMEM((2 * HEAD_DIM, 2 * T), BF16), pltpu.VMEM((1, 2 * T), F32),
                        pltpu.VMEM((1, 2 * T), F32), pltpu.VMEM((2 * HEAD_DIM, T), F32),
                        pltpu.VMEM((2 * HEAD_DIM, T), F32)],
        compiler_params=pltpu.CompilerParams(dimension_semantics=("arbitrary", "arbitrary", "arbitrary"),
                                             vmem_limit_bytes=VMEM_LIMIT),
        name="diff_attn_online" if online else "diff_attn",
    )(lamv, q, k, vt, bp, bd, sg)


def _rel_bucket_np(n):
    max_exact = NUM_BUCKETS // 2
    nf = np.maximum(n, 1).astype(np.float32)
    large = max_exact + (np.log(nf / max_exact) / math.log(MAX_DISTANCE / max_exact)
                         * (NUM_BUCKETS - max_exact)).astype(np.int32)
    large = np.minimum(large, NUM_BUCKETS - 1)
    return np.where(n < max_exact, n, large)


def _bias_body(tab_ref, idx_ref, o_ref):
    h = pl.program_id(1)
    idx = idx_ref[0]
    acc = jnp.zeros(idx.shape, F32)
    for b in range(NUM_BUCKETS):
        acc = jnp.where(idx == b, tab_ref[b, h], acc)
    o_ref[0, 0] = jnp.where(idx < 0, NEG_BIG, acc)


def _bias_tiles(tab):
    T = ATT_TILE
    c = np.arange(T)[:, None]
    r = np.arange(T)[None, :]
    idx_d = np.where(r - c >= 0, _rel_bucket_np(np.maximum(r - c, 0)), -1)
    idx = jnp.asarray(np.stack([_rel_bucket_np(r - c + T), idx_d]).astype(np.int32))
    return pl.pallas_call(
        _bias_body,
        grid=(2, N_HEADS),
        in_specs=[pl.BlockSpec(memory_space=pltpu.SMEM),
                  pl.BlockSpec((1, T, T), lambda t, h: (t, 0, 0))],
        out_specs=pl.BlockSpec((1, 1, T, T), lambda t, h: (t, h, 0, 0)),
        out_shape=jax.ShapeDtypeStruct((2, N_HEADS, T, T), F32),
        name="bias_tiles",
    )(tab, idx)


def kernel(x, rel_bias, ffn1_norm, ffn1_w_gate, ffn1_w_up, ffn1_w_down, mix_norm, w_in, conv_w, conv_b,
           conv_norm, q_norm, k_norm, lambda_q1, lambda_k1, lambda_q2, lambda_k2, subln_norm, w_out,
           ffn2_norm, ffn2_w_gate, ffn2_w_up, ffn2_w_down):
    B, S, _ = x.shape
    depth = w_in.shape[0]
    assert S % ATT_TILE == 0 and ROW_TILE == ATT_TILE and ATT_TILE >= MAX_DISTANCE

    def pack_gu(wg, wu):
        return jnp.concatenate([wg.astype(BF16), wu.astype(BF16)], axis=-1)

    grp = np.arange(GROUP_MM) // HEAD_DIM
    gsum = jnp.asarray(grp[:, None] == grp[None, :], BF16)
    tab = (rel_bias - rel_bias[NUM_BUCKETS - 1:]).astype(F32) * LOG2E
    bias = _bias_tiles(tab)
    bp, bd = bias[0], bias[1]
    row2 = lambda a: a.reshape(1, -1).astype(F32)

    xf = x.reshape(B * S, D_MODEL)
    for l in range(depth):
        lam_init = 0.8 - 0.6 * math.exp(-0.3 * l)
        qg = q_norm[l].astype(F32) * (LOG2E / math.sqrt(HEAD_DIM))
        kg = k_norm[l].astype(F32)
        qkg = jnp.concatenate([jnp.tile(qg, 2 * N_HEADS), jnp.tile(kg, 2 * N_HEADS)]).reshape(1, -1)
        score_bound = (HEAD_DIM * 1.02) * jnp.max(jnp.abs(qg)) * jnp.max(jnp.abs(kg)) + jnp.max(jnp.abs(tab))
        cw8 = jnp.broadcast_to(conv_w[l].astype(F32)[:, None, :], (CONV_K, SUBLANES, CONV_CH))
        x1, c, q, k, vt = _ffn_proj(xf, row2(ffn1_norm[l]), pack_gu(ffn1_w_gate[l], ffn1_w_up[l]),
                                   ffn1_w_down[l].astype(BF16), row2(mix_norm[l]), w_in[l].astype(BF16),
                                   gsum, qkg, cw8, row2(conv_b[l]), row2(conv_norm[l]), S)
        lamv = jnp.stack([lambda_q1[l], lambda_k1[l], lambda_q2[l], lambda_k2[l]]).astype(F32)
        sg = row2(subln_norm[l]) * (1.0 - lam_init)
        att = lax.cond(score_bound <= SAFE_EXP2_BOUND,
                       functools.partial(_diff_attn, False, lam_init),
                       functools.partial(_diff_attn, True, lam_init),
                       q.reshape(B, S, DIFF_WIDTH), k.reshape(B, S, DIFF_WIDTH), vt, bp, bd, lamv, sg)
        xf = _mix_ffn(x1, c, att.reshape(B * S, DIFF_WIDTH), w_out[l, :CONV_CH].astype(BF16),
                      w_out[l, CONV_CH:].astype(BF16), row2(ffn2_norm[l]),
                      pack_gu(ffn2_w_gate[l], ffn2_w_up[l]), ffn2_w_down[l].astype(BF16))
    return xf.reshape(B, S, D_MODEL)
```

```python
import functools
import math

import jax
import jax.numpy as jnp
import numpy as np
from jax import lax
from jax.experimental import pallas as pl
from jax.experimental.pallas import tpu as pltpu

F32 = jnp.float32
BF16 = jnp.bfloat16

D_MODEL = 1024
CONV_CH = 512
CONV_K = 31
DIFF_WIDTH = 512
HEAD_DIM = 64
N_HEADS = 4
D_FF = 2816
NUM_BUCKETS = 32
MAX_DISTANCE = 128
EPS = 1e-6
LOG2E = 1.4426950408889634
NEG_BIG = -1e30
SAFE_EXP2_BOUND = 90.0

FF_CHUNK = 256
N_FF_CHUNKS = D_FF // FF_CHUNK
ROW_TILE = 512
ATT_TILE = 512
FAR_STEP = 8
SIDE_WORK_LAG = 2
SUBLANES = 8
CONV_HALO = 32
CONV_ROWS = 32
GROUP_MM = 256
VMEM_LIMIT = 56 * 1024 * 1024


def _rms(x, g):
    ms = jnp.mean(x * x, axis=-1, keepdims=True)
    return x * lax.rsqrt(ms + EPS) * g


def _zero_after(v):
    bits = pltpu.bitcast(v[0:1, :], jnp.uint32)
    return pltpu.bitcast((bits >> 16) >> 16, F32)


def _ffn(x, g_ref, wgu_ref, wd_ref, act_ref, side_work=None):
    h = _rms(x, g_ref[...]).astype(BF16)
    zs = [None] * SIDE_WORK_LAG
    for c in range(N_FF_CHUNKS):
        lo = c * FF_CHUNK
        gate = jnp.dot(h, wgu_ref[:, lo:lo + FF_CHUNK], preferred_element_type=F32)
        up = jnp.dot(h, wgu_ref[:, D_FF + lo:D_FF + lo + FF_CHUNK], preferred_element_type=F32)
        z = zs.pop(0)
        if z is not None:
            up = up + z
        act_ref[:, c * FF_CHUNK:(c + 1) * FF_CHUNK] = (gate * jax.nn.sigmoid(gate) * up).astype(BF16)
        zs.append(side_work(c, gate) if side_work is not None else None)
    y = jnp.dot(act_ref[...], wd_ref[...], preferred_element_type=F32)
    return x + 0.5 * y


def _conv_prev_tile(ubuf, ush, cw_ref, cb_ref, cn_ref, c_ref):
    tm = c_ref.shape[0]
    base = CONV_HALO - (CONV_K - 1)
    n_pieces = N_FF_CHUNKS
    rows_per = -(-tm // (n_pieces * SUBLANES)) * SUBLANES

    def piece(p, gate):
        r0 = p * rows_per
        nr = min(rows_per, tm - r0)
        if nr <= 0:
            return None
        groups = nr // SUBLANES
        z = _zero_after(gate)
        bias = cb_ref[...] + jnp.concatenate([z] * (CONV_CH // FF_CHUNK), axis=1)
        acc = jnp.broadcast_to(bias, (groups, SUBLANES, CONV_CH))
        for j in range(CONV_K):
            a, b = divmod(base + j, SUBLANES)
            lo = r0 + a * SUBLANES
            rows = ubuf[lo:lo + nr, :] if b == 0 else ush[b - 1, lo:lo + nr, :]
            acc = acc + cw_ref[j][None] * rows.reshape(groups, SUBLANES, CONV_CH)
        c = _rms(acc.reshape(nr, CONV_CH), cn_ref[...])
        c = c * jax.nn.sigmoid(c)
        c_ref[r0:r0 + nr, :] = c.astype(c_ref.dtype)
        return _zero_after(jnp.sum(c, axis=0, keepdims=True))[:, :FF_CHUNK]

    return piece


def _ffn_proj_body(seq_tiles, x_ref, n1_ref, wgu_ref, wd_ref, nm_ref, win_ref, gsum_ref, qkg_ref,
                   cw_ref, cb_ref, cn_ref, xo_ref, c_ref, q_ref, k_ref, vt_ref, act_ref, ubuf, ush):
    i = pl.program_id(0)
    tm = x_ref.shape[0]

    @pl.when(i == 0)
    def _():
        ubuf[...] = jnp.zeros(ubuf.shape, F32)
        ush[...] = jnp.zeros(ush.shape, F32)

    conv_piece = _conv_prev_tile(ubuf, ush, cw_ref, cb_ref, cn_ref, c_ref)
    x1 = _ffn(x_ref[...], n1_ref, wgu_ref, wd_ref, act_ref, side_work=conv_piece)
    xo_ref[...] = x1
    h = _rms(x1, nm_ref[...]).astype(BF16)
    ag = jnp.dot(h, win_ref[:, 0:2 * CONV_CH], preferred_element_type=F32)
    u = ag[:, :CONV_CH] * jax.nn.sigmoid(ag[:, CONV_CH:])
    qk =jnp.dot(h, win_ref[:, 2 * CONV_CH:2 * CONV_CH + 2 * DIFF_WIDTH], preferred_element_type=F32)
    sq = (qk * qk).astype(BF16)
    ss = jnp.concatenate(
        [jnp.dot(sq[:, c * GROUP_MM:(c + 1) * GROUP_MM], gsum_ref[...], preferred_element_type=F32)
         for c in range(2 * DIFF_WIDTH // GROUP_MM)], axis=1)
    qkn = qk * lax.rsqrt(ss * (1.0 / HEAD_DIM) + EPS) * qkg_ref[...]
    q_ref[...] = qkn[:, :DIFF_WIDTH].astype(BF16)
    k_ref[...] = qkn[:, DIFF_WIDTH:].astype(BF16)
    v = jnp.dot(h, win_ref[:, 2 * CONV_CH + 2 * DIFF_WIDTH:], preferred_element_type=F32)
    vt_ref[0, :, 0] = v.T.reshape(N_HEADS, 2 * HEAD_DIM, tm).astype(BF16)

    tail = ubuf[tm:tm + CONV_HALO, :]
    ubuf[0:CONV_HALO, :] = jnp.where(i % seq_tiles == 0, 0.0, tail)
    ubuf[CONV_HALO:, :] = u
    for b in range(1, SUBLANES):
        ush[b - 1] = ubuf[b:b + ush.shape[1], :]


def _const_spec(shape):
    nd = len(shape)
    return pl.BlockSpec(shape, lambda i: (0,) * nd, pipeline_mode=pl.Buffered(1))


def _ffn_proj(x, n1, wgu, wd, nm, win, gsum, qkg, cw, cb, cn, seq):
    t = x.shape[0]
    tm = ROW_TILE
    n = t // tm
    cur = lambda w: pl.BlockSpec((tm, w), lambda i: (jnp.minimum(i, n - 1), 0))
    prev = lambda w: pl.BlockSpec((tm, w), lambda i: (jnp.maximum(i - 1, 0), 0))
    consts = (n1, wgu, wd, nm, win, gsum, qkg, cw, cb, cn)
    seq_tiles = seq // tm

    def vt_index(i):
        ii = jnp.minimum(i, n - 1)
        return (ii // seq_tiles, 0, ii % seq_tiles, 0, 0)

    return pl.pallas_call(
        functools.partial(_ffn_proj_body, seq_tiles),
        grid=(n + 1,),
        in_specs=[cur(D_MODEL)] + [_const_spec(a.shape) for a in consts],
        out_specs=[cur(D_MODEL), prev(CONV_CH), cur(DIFF_WIDTH), cur(DIFF_WIDTH),
                   pl.BlockSpec((1, N_HEADS, 1, 2 * HEAD_DIM, tm), vt_index)],
        out_shape=[jax.ShapeDtypeStruct((t, D_MODEL), F32), jax.ShapeDtypeStruct((t, CONV_CH), BF16),
                   jax.ShapeDtypeStruct((t, DIFF_WIDTH), BF16), jax.ShapeDtypeStruct((t, DIFF_WIDTH), BF16),
                   jax.ShapeDtypeStruct((t // seq, N_HEADS, seq_tiles, 2 * HEAD_DIM, tm), BF16)],
        scratch_shapes=[pltpu.VMEM((tm, D_FF), BF16), pltpu.VMEM((tm + CONV_HALO, CONV_CH), F32),
                        pltpu.VMEM((SUBLANES - 1, tm + CONV_HALO - SUBLANES, CONV_CH), F32)],
        compiler_params=pltpu.CompilerParams(dimension_semantics=("arbitrary",),
                                             vmem_limit_bytes=VMEM_LIMIT),
        name="ffn_proj",
    )(x, *consts)


def _mix_ffn_body(x_ref, c_ref, att_ref, woc_ref, wod_ref, n2_ref, wgu_ref, wd_ref, o_ref, act_ref):
    mix = (jnp.dot(c_ref[...], woc_ref[...], preferred_element_type=F32)
           + jnp.dot(att_ref[...], wod_ref[...], preferred_element_type=F32))
    x1 = x_ref[...] + mix
    o_ref[...] = _ffn(x1, n2_ref, wgu_ref, wd_ref, act_ref)


def _mix_ffn(x, c, att, woc, wod, n2, wgu, wd):
    t = x.shape[0]
    tm = ROW_TILE
    row = lambda w: pl.BlockSpec((tm, w), lambda i: (i, 0))
    consts = (woc, wod, n2, wgu, wd)
    return pl.pallas_call(
        _mix_ffn_body,
        grid=(t // tm,),
        in_specs=[row(D_MODEL), row(CONV_CH), row(DIFF_WIDTH)] + [_const_spec(a.shape) for a in consts],
        out_specs=row(D_MODEL),
        out_shape=jax.ShapeDtypeStruct((t, D_MODEL), F32),
        scratch_shapes=[pltpu.VMEM((tm, D_FF), BF16)],
        compiler_params=pltpu.CompilerParams(dimension_semantics=("arbitrary",),
                                             vmem_limit_bytes=VMEM_LIMIT),
        name="mix_ffn",
    )(x, c, att, *consts)


def _attn_body(lam_init, online, lamv_ref, q_ref, k_ref, vt_ref, bp_ref, bd_ref, sg_ref, o_ref,
               qbd_ref, m_ref, l_ref, acc1_ref, acc2_ref):
    T = ATT_TILE
    i = pl.program_id(2)

    qt = q_ref[0].astype(F32).T
    row = lax.broadcasted_iota(jnp.int32, qt.shape, 0)
    qbd_ref[:, :T] = jnp.where(row < HEAD_DIM, qt, 0.0).astype(BF16)
    qbd_ref[:, T:] = jnp.where(row >= HEAD_DIM, qt, 0.0).astype(BF16)
    m_ref[...] = jnp.full(m_ref.shape, NEG_BIG, F32)
    l_ref[...] = jnp.zeros(l_ref.shape, F32)
    acc1_ref[...] = jnp.zeros(acc1_ref.shape, F32)
    acc2_ref[...] = jnp.zeros(acc2_ref.shape, F32)

    def block(j, nblk, bias_refs=()):
        kb = k_ref[0, pl.ds(pl.multiple_of(j * T, T), nblk * T), :]
        s = jnp.dot(kb, qbd_ref[...], preferred_element_type=F32)
        if bias_refs:
            b = jnp.concatenate([r[0] for r in bias_refs], axis=0)
            s = s + jnp.concatenate([b, b], axis=1)
        vtb = jnp.concatenate([vt_ref[0, 0, j + t] for t in range(nblk)], axis=1)
        if not online:
            p = jnp.exp2(s)
            l_ref[...] += jnp.sum(p, axis=0, keepdims=True)
            pb = p.astype(BF16)
            acc1_ref[...] += jnp.dot(vtb, pb[:, :T], preferred_element_type=F32)
            acc2_ref[...] += jnp.dot(vtb, pb[:, T:], preferred_element_type=F32)
            return
        m_old = m_ref[...]
        m_new = jnp.maximum(m_old, jnp.max(s, axis=0, keepdims=True))
        alpha = jnp.exp2(m_old - m_new)
        p = jnp.exp2(s - m_new)
        l_ref[...] = alpha * l_ref[...] + jnp.sum(p, axis=0, keepdims=True)
        m_ref[...] = m_new
        pb = p.astype(BF16)
        acc1_ref[...] = alpha[:, :T] * acc1_ref[...] + jnp.dot(vtb, pb[:, :T], preferred_element_type=F32)
        acc2_ref[...] = alpha[:, T:] * acc2_ref[...] + jnp.dot(vtb, pb[:, T:], preferred_element_type=F32)

    n_far = jnp.maximum(i - 1, 0)

    def far_step(jj, carry):
        block(FAR_STEP * jj, FAR_STEP)
        return carry

    lax.fori_loop(0, n_far // FAR_STEP, far_step, 0)
    done = (n_far // FAR_STEP) * FAR_STEP
    size = FAR_STEP // 2
    while size >= 1:
        take = (n_far - done) >= size

        @pl.when(take)
        def _(done=done, size=size):
            block(done, size)

        done = done + jnp.where(take, size, 0)
        size //= 2

    @pl.when(i > 0)
    def _():
        block(i - 1, 2, (bp_ref, bd_ref))

    @pl.when(i == 0)
    def _():
        block(i, 1, (bd_ref,))

    lv = lamv_ref[...]
    lam = (jnp.exp(jnp.sum(lv[0:1] * lv[1:2], axis=1, keepdims=True))
           - jnp.exp(jnp.sum(lv[2:3] * lv[3:4], axis=1, keepdims=True)) + lam_init)
    l = l_ref[...]
    d = acc1_ref[...] / l[:, :T] - lam * (acc2_ref[...] / l[:, T:])
    ms = jnp.mean(d * d, axis=0, keepdims=True)
    y = d * lax.rsqrt(ms + EPS)
    o_ref[0] = (y.T * sg_ref[...]).astype(o_ref.dtype)


def _diff_attn(online, lam_init, q, k, vt, bp, bd, lamv, sg):
    b, s, _ = q.shape
    T = ATT_TILE
    nq = s // T
    return pl.pallas_call(
        functools.partial(_attn_body, lam_init, online),
        grid=(b, N_HEADS, nq),
        in_specs=[
            pl.BlockSpec(lamv.shape, lambda bi, h, i: (0, 0)),
            pl.BlockSpec((1, T, 2 * HEAD_DIM), lambda bi, h, i: (bi, i, h)),
            pl.BlockSpec((1, s, 2 * HEAD_DIM), lambda bi, h, i: (bi, 0, h)),
            pl.BlockSpec((1, 1, nq, 2 * HEAD_DIM, T), lambda bi, h, i: (bi, h, 0, 0, 0)),
            pl.BlockSpec((1, T, T), lambda bi, h, i: (h, 0, 0)),
            pl.BlockSpec((1, T, T), lambda bi, h, i: (h, 0, 0)),
            pl.BlockSpec(sg.shape, lambda bi, h, i: (0, 0)),
        ],
        out_specs=pl.BlockSpec((1, T, 2 * HEAD_DIM), lambda bi, h, i: (bi, i, h)),
        out_shape=jax.ShapeDtypeStruct((b, s, DIFF_WIDTH), BF16),
        scratch_shapes=[pltpu.VMEM((2 * HEAD_DIM, 2 * T), BF16), pltpu.VMEM((1, 2 * T), F32),
                        pltpu.VMEM((1, 2 * T), F32), pltpu.VMEM((2 * HEAD_DIM, T), F32),
                        pltpu.VMEM((2 * HEAD_DIM, T), F32)],
        compiler_params=pltpu.CompilerParams(dimension_semantics=("arbitrary", "arbitrary", "arbitrary"),
                                             vmem_limit_bytes=VMEM_LIMIT),
        name="diff_attn_online" if online else "diff_attn",
    )(lamv, q, k, vt, bp, bd, sg)


def _rel_bucket_np(n):
    max_exact = NUM_BUCKETS // 2
    nf = np.maximum(n, 1).astype(np.float32)
    large = max_exact + (np.log(nf / max_exact) / math.log(MAX_DISTANCE / max_exact)
                         * (NUM_BUCKETS - max_exact)).astype(np.int32)
    large = np.minimum(large, NUM_BUCKETS - 1)
    return np.where(n < max_exact, n, large)


def _bias_body(tab_ref, idx_ref, o_ref):
    h = pl.program_id(1)
    idx = idx_ref[0]
    acc = jnp.zeros(idx.shape, F32)
    for b in range(NUM_BUCKETS):
        acc = jnp.where(idx == b, tab_ref[b, h], acc)
    o_ref[0, 0] = jnp.where(idx < 0, NEG_BIG, acc)


def _bias_tiles(tab):
    T = ATT_TILE
    c = np.arange(T)[:, None]
    r = np.arange(T)[None, :]
    idx_d = np.where(r - c >= 0, _rel_bucket_np(np.maximum(r - c, 0)), -1)
    idx = jnp.asarray(np.stack([_rel_bucket_np(r - c + T), idx_d]).astype(np.int32))
    return pl.pallas_call(
        _bias_body,
        grid=(2, N_HEADS),
        in_specs=[pl.BlockSpec(memory_space=pltpu.SMEM),
                  pl.BlockSpec((1, T, T), lambda t, h: (t, 0, 0))],
        out_specs=pl.BlockSpec((1, 1, T, T), lambda t, h: (t, h, 0, 0)),
        out_shape=jax.ShapeDtypeStruct((2, N_HEADS, T, T), F32),
        name="bias_tiles",
    )(tab, idx)


def kernel(x, rel_bias, ffn1_norm, ffn1_w_gate, ffn1_w_up, ffn1_w_down, mix_norm, w_in, conv_w, conv_b,
           conv_norm, q_norm, k_norm, lambda_q1, lambda_k1, lambda_q2, lambda_k2, subln_norm, w_out,
           ffn2_norm, ffn2_w_gate, ffn2_w_up, ffn2_w_down):
    B, S, _ = x.shape
    depth = w_in.shape[0]
    assert S % ATT_TILE == 0 and ROW_TILE == ATT_TILE and ATT_TILE >= MAX_DISTANCE

    def pack_gu(wg, wu):
        return jnp.concatenate([wg.astype(BF16), wu.astype(BF16)], axis=-1)

    grp = np.arange(GROUP_MM) // HEAD_DIM
    gsum = jnp.asarray(grp[:, None] == grp[None, :], BF16)
    tab = (rel_bias - rel_bias[NUM_BUCKETS - 1:]).astype(F32) * LOG2E
    bias = _bias_tiles(tab)
    bp, bd = bias[0], bias[1]
    row2 = lambda a: a.reshape(1, -1).astype(F32)

    xf = x.reshape(B * S, D_MODEL)
    for l in range(depth):
        lam_init = 0.8 - 0.6 * math.exp(-0.3 * l)
        qg = q_norm[l].astype(F32) * (LOG2E / math.sqrt(HEAD_DIM))
        kg = k_norm[l].astype(F32)
        qkg = jnp.concatenate([jnp.tile(qg, 2 * N_HEADS), jnp.tile(kg, 2 * N_HEADS)]).reshape(1, -1)
        score_bound = (HEAD_DIM * 1.02) * jnp.max(jnp.abs(qg)) * jnp.max(jnp.abs(kg)) + jnp.max(jnp.abs(tab))
        cw8 = jnp.broadcast_to(conv_w[l].astype(F32)[:, None, :], (CONV_K, SUBLANES, CONV_CH))
        x1, c, q, k, vt = _ffn_proj(xf, row2(ffn1_norm[l]), pack_gu(ffn1_w_gate[l], ffn1_w_up[l]),
                                   ffn1_w_down[l].astype(BF16), row2(mix_norm[l]), w_in[l].astype(BF16),
                                   gsum, qkg, cw8, row2(conv_b[l]), row2(conv_norm[l]), S)
        lamv = jnp.stack([lambda_q1[l], lambda_k1[l], lambda_q2[l], lambda_k2[l]]).astype(F32)
        sg = row2(subln_norm[l]) * (1.0 - lam_init)
        att = lax.cond(score_bound <= SAFE_EXP2_BOUND,
                       functools.partial(_diff_attn, False, lam_init),
                       functools.partial(_diff_attn, True, lam_init),
                       q.reshape(B, S, DIFF_WIDTH), k.reshape(B, S, DIFF_WIDTH), vt, bp, bd, lamv, sg)
        xf = _mix_ffn(x1, c, att.reshape(B * S, DIFF_WIDTH), w_out[l, :CONV_CH].astype(BF16),
                      w_out[l, CONV_CH:].astype(BF16), row2(ffn2_norm[l]),
                      pack_gu(ffn2_w_gate[l], ffn2_w_up[l]), ffn2_w_down[l].astype(BF16))
    return xf.reshape(B, S, D_MODEL)
```

```python
import functools
import math

import jax
import jax.numpy as jnp
import numpy as np
from jax import lax
from jax.experimental import pallas as pl
from jax.experimental.pallas import tpu as pltpu

F32 = jnp.float32
BF16 = jnp.bfloat16

D_MODEL = 1024
CONV_CH = 512
CONV_K = 31
DIFF_WIDTH = 512
HEAD_DIM = 64
N_HEADS = 4
D_FF = 2816
NUM_BUCKETS = 32
MAX_DISTANCE = 128
EPS = 1e-6
LOG2E = 1.4426950408889634
NEG_BIG = -1e30
SAFE_EXP2_BOUND = 90.0

FF_CHUNK = 256
N_FF_CHUNKS = D_FF // FF_CHUNK
ROW_TILE = 512
ATT_TILE = 512
Q_TILE = 1024
FAR_STEP = 4
SIDE_WORK_LAG = 2
SUBLANES = 8
CONV_HALO = 32
CONV_ROWS = 32
GROUP_MM = 256
VMEM_LIMIT = 56 * 1024 * 1024


def _rms(x, g):
    ms = jnp.mean(x * x, axis=-1, keepdims=True)
    return x * lax.rsqrt(ms + EPS) * g


def _zero_after(v):
    bits = pltpu.bitcast(v[0:1, :], jnp.uint32)
    return pltpu.bitcast((bits >> 16) >> 16, F32)


def _ffn(x, g_ref, wgu_ref, wd_ref, act_ref, side_work=None):
    h = _rms(x, g_ref[...]).astype(BF16)
    zs = [None] * SIDE_WORK_LAG
    for c in range(N_FF_CHUNKS):
        lo = c * FF_CHUNK
        gate = jnp.dot(h, wgu_ref[:, lo:lo + FF_CHUNK], preferred_element_type=F32)
        up = jnp.dot(h, wgu_ref[:, D_FF + lo:D_FF + lo + FF_CHUNK], preferred_element_type=F32)
        z = zs.pop(0)
        if z is not None:
            up = up + z
        act_ref[:, c * FF_CHUNK:(c + 1) * FF_CHUNK] = (gate * jax.nn.sigmoid(gate) * up).astype(BF16)
        zs.append(side_work(c, gate) if side_work is not None else None)
    y = jnp.dot(act_ref[...], wd_ref[...], preferred_element_type=F32)
    return x + 0.5 * y


def _conv_prev_tile(ubuf, ush, cw_ref, cb_ref, cn_ref, c_ref):
    tm = c_ref.shape[0]
    base = CONV_HALO - (CONV_K - 1)
    n_pieces = N_FF_CHUNKS
    rows_per = -(-tm // (n_pieces * SUBLANES)) * SUBLANES

    def piece(p, gate):
        r0 = p * rows_per
        nr = min(rows_per, tm - r0)
        if nr <= 0:
            return None
        groups = nr // SUBLANES
        z = _zero_after(gate)
        bias = cb_ref[...] + jnp.concatenate([z] * (CONV_CH // FF_CHUNK), axis=1)
        acc = jnp.broadcast_to(bias, (groups, SUBLANES, CONV_CH))
        for j in range(CONV_K):
            a, b = divmod(base + j, SUBLANES)
            lo = r0 + a * SUBLANES
            rows = ubuf[lo:lo + nr, :] if b == 0 else ush[b - 1, lo:lo + nr, :]
            acc = acc + cw_ref[j][None] * rows.reshape(groups, SUBLANES, CONV_CH)
        c = _rms(acc.reshape(nr, CONV_CH), cn_ref[...])
        c = c * jax.nn.sigmoid(c)
        c_ref[r0:r0 + nr, :] = c.astype(c_ref.dtype)
        return _zero_after(jnp.sum(c, axis=0, keepdims=True))[:, :FF_CHUNK]

    return piece


def _ffn_proj_body(seq_tiles, x_ref, n1_ref, wgu_ref, wd_ref, nm_ref, win_ref, gsum_ref, qkg_ref,
                   cw_ref, cb_ref, cn_ref, xo_ref, c_ref, q_ref, k_ref, vt_ref, act_ref, ubuf, ush):
    i = pl.program_id(0)
    tm = x_ref.shape[0]

    @pl.when(i == 0)
    def _():
        ubuf[...] = jnp.zeros(ubuf.shape, F32)
        ush[...] = jnp.zeros(ush.shape, F32)

    conv_piece = _conv_prev_tile(ubuf, ush, cw_ref, cb_ref, cn_ref, c_ref)
    x1 = _ffn(x_ref[...], n1_ref, wgu_ref, wd_ref, act_ref, side_work=conv_piece)
    xo_ref[...] = x1
    h = _rms(x1, nm_ref[...]).astype(BF16)
    ag = jnp.dot(h, win_ref[:, 0:2 * CONV_CH], preferred_element_type=F32)
    u = ag[:, :CONV_CH] * jax.nn.sigmoid(ag[:, CONV_CH:])
    qk =jnp.dot(h, win_ref[:, 2 * CONV_CH:2 * CONV_CH + 2 * DIFF_WIDTH], preferred_element_type=F32)
    sq = (qk * qk).astype(BF16)
    ss = jnp.concatenate(
        [jnp.dot(sq[:, c * GROUP_MM:(c + 1) * GROUP_MM], gsum_ref[...], preferred_element_type=F32)
         for c in range(2 * DIFF_WIDTH // GROUP_MM)], axis=1)
    qkn = qk * lax.rsqrt(ss * (1.0 / HEAD_DIM) + EPS) * qkg_ref[...]
    q_ref[...] = qkn[:, :DIFF_WIDTH].astype(BF16)
    k_ref[...] = qkn[:, DIFF_WIDTH:].astype(BF16)
    v = jnp.dot(h, win_ref[:, 2 * CONV_CH + 2 * DIFF_WIDTH:], preferred_element_type=F32)
    vt_ref[0, :, 0] = v.T.reshape(N_HEADS, 2 * HEAD_DIM, tm).astype(BF16)

    tail = ubuf[tm:tm + CONV_HALO, :]
    ubuf[0:CONV_HALO, :] = jnp.where(i % seq_tiles == 0, 0.0, tail)
    ubuf[CONV_HALO:, :] = u
    for b in range(1, SUBLANES):
        ush[b - 1] = ubuf[b:b + ush.shape[1], :]


def _const_spec(shape):
    nd = len(shape)
    return pl.BlockSpec(shape, lambda i: (0,) * nd, pipeline_mode=pl.Buffered(1))


def _ffn_proj(x, n1, wgu, wd, nm, win, gsum, qkg, cw, cb, cn, seq):
    t = x.shape[0]
    tm = ROW_TILE
    n = t // tm
    cur = lambda w: pl.BlockSpec((tm, w), lambda i: (jnp.minimum(i, n - 1), 0))
    prev = lambda w: pl.BlockSpec((tm, w), lambda i: (jnp.maximum(i - 1, 0), 0))
    consts = (n1, wgu, wd, nm, win, gsum, qkg, cw, cb, cn)
    seq_tiles = seq // tm

    def vt_index(i):
        ii = jnp.minimum(i, n - 1)
        return (ii // seq_tiles, 0, ii % seq_tiles, 0, 0)

    return pl.pallas_call(
        functools.partial(_ffn_proj_body, seq_tiles),
        grid=(n + 1,),
        in_specs=[cur(D_MODEL)] + [_const_spec(a.shape) for a in consts],
        out_specs=[cur(D_MODEL), prev(CONV_CH), cur(DIFF_WIDTH), cur(DIFF_WIDTH),
                   pl.BlockSpec((1, N_HEADS, 1, 2 * HEAD_DIM, tm), vt_index)],
        out_shape=[jax.ShapeDtypeStruct((t, D_MODEL), F32), jax.ShapeDtypeStruct((t, CONV_CH), BF16),
                   jax.ShapeDtypeStruct((t, DIFF_WIDTH), BF16), jax.ShapeDtypeStruct((t, DIFF_WIDTH), BF16),
                   jax.ShapeDtypeStruct((t // seq, N_HEADS, seq_tiles, 2 * HEAD_DIM, tm), BF16)],
        scratch_shapes=[pltpu.VMEM((tm, D_FF), BF16), pltpu.VMEM((tm + CONV_HALO, CONV_CH), F32),
                        pltpu.VMEM((SUBLANES - 1, tm + CONV_HALO - SUBLANES, CONV_CH), F32)],
        compiler_params=pltpu.CompilerParams(dimension_semantics=("arbitrary",),
                                             vmem_limit_bytes=VMEM_LIMIT),
        name="ffn_proj",
    )(x, *consts)


def _mix_ffn_body(x_ref, c_ref, att_ref, woc_ref, wod_ref, n2_ref, wgu_ref, wd_ref, o_ref, act_ref):
    mix = (jnp.dot(c_ref[...], woc_ref[...], preferred_element_type=F32)
           + jnp.dot(att_ref[...], wod_ref[...], preferred_element_type=F32))
    x1 = x_ref[...] + mix
    o_ref[...] = _ffn(x1, n2_ref, wgu_ref, wd_ref, act_ref)


def _mix_ffn(x, c, att, woc, wod, n2, wgu, wd):
    t = x.shape[0]
    tm = ROW_TILE
    row = lambda w: pl.BlockSpec((tm, w), lambda i: (i, 0))
    consts = (woc, wod, n2, wgu, wd)
    return pl.pallas_call(
        _mix_ffn_body,
        grid=(t // tm,),
        in_specs=[row(D_MODEL), row(CONV_CH), row(DIFF_WIDTH)] + [_const_spec(a.shape) for a in consts],
        out_specs=row(D_MODEL),
        out_shape=jax.ShapeDtypeStruct((t, D_MODEL), F32),
        scratch_shapes=[pltpu.VMEM((tm, D_FF), BF16)],
        compiler_params=pltpu.CompilerParams(dimension_semantics=("arbitrary",),
                                             vmem_limit_bytes=VMEM_LIMIT),
        name="mix_ffn",
    )(x, c, att, *consts)


def _attn_body(lam_init, online, lamv_ref, q_ref, k_ref, vt_ref, bias_ref, sg_ref, o_ref,
               qbd_ref, m_ref, l_ref, acc1_ref, acc2_ref):
    T = Q_TILE
    TK = ATT_TILE
    ratio = T // TK
    i = pl.program_id(2)

    qt = q_ref[0].astype(F32).T
    row = lax.broadcasted_iota(jnp.int32, qt.shape, 0)
    qbd_ref[:, :T] = jnp.where(row < HEAD_DIM, qt, 0.0).astype(BF16)
    qbd_ref[:, T:] = jnp.where(row >= HEAD_DIM, qt, 0.0).astype(BF16)
    m_ref[...] = jnp.full(m_ref.shape, NEG_BIG, F32)
    l_ref[...] = jnp.zeros(l_ref.shape, F32)
    acc1_ref[...] = jnp.zeros(acc1_ref.shape, F32)
    acc2_ref[...] = jnp.zeros(acc2_ref.shape, F32)

    def block(j, nblk, first_bias=None, plain=0):
        kb = k_ref[0, pl.ds(pl.multiple_of(j * TK, TK), nblk * TK), :]
        s = jnp.dot(kb, qbd_ref[...], preferred_element_type=F32)
        if first_bias is not None:
            nb = nblk - plain
            b = bias_ref[first_bias:first_bias + nb, 0].reshape(nb * TK, T)
            biased = s[plain * TK:] + jnp.concatenate([b, b], axis=1)
            s = biased if plain == 0 else jnp.concatenate([s[:plain * TK], biased], axis=0)
        vtb = jnp.concatenate([vt_ref[0, 0, j + t] for t in range(nblk)], axis=1)
        if not online:
            p = jnp.exp2(s)
            l_ref[...] += jnp.sum(p, axis=0, keepdims=True)
            pb = p.astype(BF16)
            acc1_ref[...] += jnp.dot(vtb, pb[:, :T], preferred_element_type=F32)
            acc2_ref[...] += jnp.dot(vtb, pb[:, T:], preferred_element_type=F32)
            return
        m_old = m_ref[...]
        m_new = jnp.maximum(m_old, jnp.max(s, axis=0, keepdims=True))
        alpha = jnp.exp2(m_old - m_new)
        p = jnp.exp2(s - m_new)
        l_ref[...] = alpha * l_ref[...] + jnp.sum(p, axis=0, keepdims=True)
        m_ref[...] = m_new
        pb = p.astype(BF16)
        acc1_ref[...] = alpha[:, :T] * acc1_ref[...] + jnp.dot(vtb, pb[:, :T], preferred_element_type=F32)
        acc2_ref[...] = alpha[:, T:] * acc2_ref[...] + jnp.dot(vtb, pb[:, T:], preferred_element_type=F32)

    n_far = jnp.maximum(ratio * i - 2, 0)

    def far_step(jj, carry):
        block(FAR_STEP * jj, FAR_STEP)
        return carry

    lax.fori_loop(0, n_far // FAR_STEP, far_step, 0)
    done = (n_far // FAR_STEP) * FAR_STEP
    size = FAR_STEP // 2
    while size >= 2:
        take = (n_far - done) >= size

        @pl.when(take)
        def _(done=done, size=size):
            block(done, size)

        done = done + jnp.where(take, size, 0)
        size //= 2

    @pl.when(i > 0)
    def _():
        block(ratio * i - 2, ratio + 2, 0, plain=1)

    @pl.when(i == 0)
    def _():
        block(0, ratio, 1)

    lv = lamv_ref[...]
    lam = (jnp.exp(jnp.sum(lv[0:1] * lv[1:2], axis=1, keepdims=True))
           - jnp.exp(jnp.sum(lv[2:3] * lv[3:4], axis=1, keepdims=True)) + lam_init)
    l = l_ref[...]
    d = acc1_ref[...] / l[:, :T] - lam * (acc2_ref[...] / l[:, T:])
    ms = jnp.mean(d * d, axis=0, keepdims=True)
    y = d * lax.rsqrt(ms + EPS)
    o_ref[0] = (y.T * sg_ref[...]).astype(o_ref.dtype)


def _diff_attn(online, lam_init, q, k, vt, bias, lamv, sg):
    b, s, _ = q.shape
    T = Q_TILE
    TK = ATT_TILE
    nq = s // T
    return pl.pallas_call(
        functools.partial(_attn_body, lam_init, online),
        grid=(b, N_HEADS, nq),
        in_specs=[
            pl.BlockSpec(lamv.shape, lambda bi, h, i: (0, 0)),
            pl.BlockSpec((1, T, 2 * HEAD_DIM), lambda bi, h, i: (bi, i, h)),
            pl.BlockSpec((1, s, 2 * HEAD_DIM), lambda bi, h, i: (bi, 0, h)),
            pl.BlockSpec((1, 1, s // TK, 2 * HEAD_DIM, TK), lambda bi, h, i: (bi, h, 0, 0, 0)),
            pl.BlockSpec((bias.shape[0], 1, TK, T), lambda bi, h, i: (0, h, 0, 0),
                         pipeline_mode=pl.Buffered(1)),
            pl.BlockSpec(sg.shape, lambda bi, h, i: (0, 0)),
        ],
        out_specs=pl.BlockSpec((1, T, 2 * HEAD_DIM), lambda bi, h, i: (bi, i, h)),
        out_shape=jax.ShapeDtypeStruct((b, s, DIFF_WIDTH), BF16),
        scratch_shapes=[pltpu.VMEM((2 * HEAD_DIM, 2 * T), BF16), pltpu.VMEM((1, 2 * T), F32),
                        pltpu.VMEM((1, 2 * T), F32), pltpu.VMEM((2 * HEAD_DIM, T), F32),
                        pltpu.VMEM((2 * HEAD_DIM, T), F32)],
        compiler_params=pltpu.CompilerParams(dimension_semantics=("arbitrary", "arbitrary", "arbitrary"),
                                             vmem_limit_bytes=VMEM_LIMIT),
        name="diff_attn_online" if online else "diff_attn",
    )(lamv, q, k, vt, bias, sg)


def _rel_bucket_np(n):
    max_exact = NUM_BUCKETS // 2
    nf = np.maximum(n, 1).astype(np.float32)
    large = max_exact + (np.log(nf / max_exact) / math.log(MAX_DISTANCE / max_exact)
                         * (NUM_BUCKETS - max_exact)).astype(np.int32)
    large = np.minimum(large, NUM_BUCKETS - 1)
    return np.where(n < max_exact, n, large)


def _bias_body(tab_ref, idx_ref, o_ref):
    h = pl.program_id(1)
    idx = idx_ref[0]
    acc = jnp.zeros(idx.shape, F32)
    for b in range(NUM_BUCKETS):
        acc = jnp.where(idx == b, tab_ref[b, h], acc)
    o_ref[0, 0] = jnp.where(idx < 0, NEG_BIG, acc)


def _bias_tiles(tab):
    T, TK = Q_TILE, ATT_TILE
    n_tiles = T // TK + 1
    c = np.arange(TK)[:, None]
    r = np.arange(T)[None, :]
    dist = [r - c + TK * (1 - m) for m in range(n_tiles)]
    idx = jnp.asarray(np.stack([np.where(d >= 0, _rel_bucket_np(np.maximum(d, 0)), -1)
                                for d in dist]).astype(np.int32))
    return pl.pallas_call(
        _bias_body,
        grid=(n_tiles, N_HEADS),
        in_specs=[pl.BlockSpec(memory_space=pltpu.SMEM),
                  pl.BlockSpec((1, TK, T), lambda t, h: (t, 0, 0))],
        out_specs=pl.BlockSpec((1, 1, TK, T), lambda t, h: (t, h, 0, 0)),
        out_shape=jax.ShapeDtypeStruct((n_tiles, N_HEADS, TK, T), F32),
        name="bias_tiles",
    )(tab, idx)


def kernel(x, rel_bias, ffn1_norm, ffn1_w_gate, ffn1_w_up, ffn1_w_down, mix_norm, w_in, conv_w, conv_b,
           conv_norm, q_norm, k_norm, lambda_q1, lambda_k1, lambda_q2, lambda_k2, subln_norm, w_out,
           ffn2_norm, ffn2_w_gate, ffn2_w_up, ffn2_w_down):
    B, S, _ = x.shape
    depth = w_in.shape[0]
    assert S % Q_TILE == 0 and Q_TILE == 2 * ATT_TILE and ROW_TILE == ATT_TILE and ATT_TILE >= MAX_DISTANCE

    def pack_gu(wg, wu):
        return jnp.concatenate([wg.astype(BF16), wu.astype(BF16)], axis=-1)

    grp = np.arange(GROUP_MM) // HEAD_DIM
    gsum = jnp.asarray(grp[:, None] == grp[None, :], BF16)
    tab = (rel_bias - rel_bias[NUM_BUCKETS - 1:]).astype(F32) * LOG2E
    bias = _bias_tiles(tab)
    row2 = lambda a: a.reshape(1, -1).astype(F32)

    xf = x.reshape(B * S, D_MODEL)
    for l in range(depth):
        lam_init = 0.8 - 0.6 * math.exp(-0.3 * l)
        qg = q_norm[l].astype(F32) * (LOG2E / math.sqrt(HEAD_DIM))
        kg = k_norm[l].astype(F32)
        qkg = jnp.concatenate([jnp.tile(qg, 2 * N_HEADS), jnp.tile(kg, 2 * N_HEADS)]).reshape(1, -1)
        score_bound = (HEAD_DIM * 1.02) * jnp.max(jnp.abs(qg)) * jnp.max(jnp.abs(kg)) + jnp.max(jnp.abs(tab))
        cw8 = jnp.broadcast_to(conv_w[l].astype(F32)[:, None, :], (CONV_K, SUBLANES, CONV_CH))
        x1, c, q, k, vt = _ffn_proj(xf, row2(ffn1_norm[l]), pack_gu(ffn1_w_gate[l], ffn1_w_up[l]),
                                   ffn1_w_down[l].astype(BF16), row2(mix_norm[l]), w_in[l].astype(BF16),
                                   gsum, qkg, cw8, row2(conv_b[l]), row2(conv_norm[l]), S)
        lamv = jnp.stack([lambda_q1[l], lambda_k1[l], lambda_q2[l], lambda_k2[l]]).astype(F32)
        sg = row2(subln_norm[l]) * (1.0 - lam_init)
        att = lax.cond(score_bound <= SAFE_EXP2_BOUND,
                       functools.partial(_diff_attn, False, lam_init),
                       functools.partial(_diff_attn, True, lam_init),
                       q.reshape(B, S, DIFF_WIDTH), k.reshape(B, S, DIFF_WIDTH), vt, bias, lamv, sg)
        xf = _mix_ffn(x1, c, att.reshape(B * S, DIFF_WIDTH), w_out[l, :CONV_CH].astype(BF16),
                      w_out[l, CONV_CH:].astype(BF16), row2(ffn2_norm[l]),
                      pack_gu(ffn2_w_gate[l], ffn2_w_up[l]), ffn2_w_down[l].astype(BF16))
    return xf.reshape(B, S, D_MODEL)
```

```python
import functools
import math

import jax
import jax.numpy as jnp
import numpy as np
from jax import lax
from jax.experimental import pallas as pl
from jax.experimental.pallas import tpu as pltpu

F32 = jnp.float32
BF16 = jnp.bfloat16

D_MODEL = 1024
CONV_CH = 512
CONV_K = 31
DIFF_WIDTH = 512
HEAD_DIM = 64
N_HEADS = 4
D_FF = 2816
NUM_BUCKETS = 32
MAX_DISTANCE = 128
EPS = 1e-6
LOG2E = 1.4426950408889634
NEG_BIG = -1e30
SAFE_EXP2_BOUND = 90.0

FF_CHUNK = 256
N_FF_CHUNKS = D_FF // FF_CHUNK
ROW_TILE = 512
ATT_TILE = 512
Q_TILE = 1024
FAR_STEP = 4
SIDE_WORK_LAG = 2
SUBLANES = 8
CONV_HALO = 32
CONV_ROWS = 24
GROUP_MM = 256
VMEM_LIMIT = 56 * 1024 * 1024


def _rms(x, g):
    ms = jnp.mean(x * x, axis=-1, keepdims=True)
    return x * lax.rsqrt(ms + EPS) * g


def _zero_after(v):
    bits = pltpu.bitcast(v[0:1, :], jnp.uint32)
    return pltpu.bitcast((bits >> 16) >> 16, F32)


def _ffn(x, g_ref, wgu_ref, wd_ref, act_ref, side_work=None):
    h = _rms(x, g_ref[...]).astype(BF16)
    zs = [None] * SIDE_WORK_LAG
    for c in range(N_FF_CHUNKS):
        lo = c * FF_CHUNK
        gate = jnp.dot(h, wgu_ref[:, lo:lo + FF_CHUNK], preferred_element_type=F32)
        up = jnp.dot(h, wgu_ref[:, D_FF + lo:D_FF + lo + FF_CHUNK], preferred_element_type=F32)
        z = zs.pop(0)
        if z is not None:
            up = up + z
        act_ref[:, c * FF_CHUNK:(c + 1) * FF_CHUNK] = (gate * jax.nn.sigmoid(gate) * up).astype(BF16)
        zs.append(side_work(c, gate) if side_work is not None else None)
    y = jnp.dot(act_ref[...], wd_ref[...], preferred_element_type=F32)
    return x + 0.5 * y


def _conv_prev_tile(ubuf, ush, cw_ref, cb_ref, cn_ref, c_ref):
    tm = c_ref.shape[0]
    base = CONV_HALO - (CONV_K - 1)
    n_pieces = N_FF_CHUNKS
    rows_per = -(-tm // (n_pieces * SUBLANES)) * SUBLANES

    def rows_after(r0, nr, z):
        groups = nr // SUBLANES
        bias = cb_ref[...] + z
        acc = jnp.broadcast_to(bias, (groups, SUBLANES, CONV_CH))
        for j in range(CONV_K):
            a, b = divmod(base + j, SUBLANES)
            lo = r0 + a * SUBLANES
            rows = ubuf[lo:lo + nr, :] if b == 0 else ush[b - 1, lo:lo + nr, :]
            acc = acc + cw_ref[j][None] * rows.reshape(groups, SUBLANES, CONV_CH)
        c = _rms(acc.reshape(nr, CONV_CH), cn_ref[...])
        c = c * jax.nn.sigmoid(c)
        c_ref[r0:r0 + nr, :] = c.astype(c_ref.dtype)
        return _zero_after(jnp.sum(c, axis=0, keepdims=True))

    def piece(p, gate):
        zg = _zero_after(gate)
        z = jnp.concatenate([zg] * (CONV_CH // FF_CHUNK), axis=1)
        r0 = p * rows_per
        end = min(r0 + rows_per, tm)
        while r0 < end:
            nr = min(CONV_ROWS, end - r0)
            z = rows_after(r0, nr, z)
            r0 += nr
        return z[:, :FF_CHUNK]

    return piece


def _ffn_proj_body(seq_tiles, x_ref, n1_ref, wgu_ref, wd_ref, nm_ref, win_ref, gsum_ref, qkg_ref,
                   cw_ref, cb_ref, cn_ref, xo_ref, c_ref, q_ref, k_ref, vt_ref, act_ref, ubuf, ush):
    i = pl.program_id(0)
    tm = x_ref.shape[0]

    @pl.when(i == 0)
    def _():
        ubuf[...] = jnp.zeros(ubuf.shape, F32)
        ush[...] = jnp.zeros(ush.shape, F32)

    conv_piece = _conv_prev_tile(ubuf, ush, cw_ref, cb_ref, cn_ref, c_ref)
    x1 = _ffn(x_ref[...], n1_ref, wgu_ref, wd_ref, act_ref, side_work=conv_piece)
    xo_ref[...] = x1
    h = _rms(x1, nm_ref[...]).astype(BF16)
    ag = jnp.dot(h, win_ref[:, 0:2 * CONV_CH], preferred_element_type=F32)
    u = ag[:, :CONV_CH] * jax.nn.sigmoid(ag[:, CONV_CH:])
    qk =jnp.dot(h, win_ref[:, 2 * CONV_CH:2 * CONV_CH + 2 * DIFF_WIDTH], preferred_element_type=F32)
    sq = (qk * qk).astype(BF16)
    ss = jnp.concatenate(
        [jnp.dot(sq[:, c * GROUP_MM:(c + 1) * GROUP_MM], gsum_ref[...], preferred_element_type=F32)
         for c in range(2 * DIFF_WIDTH // GROUP_MM)], axis=1)
    qkn = qk * lax.rsqrt(ss * (1.0 / HEAD_DIM) + EPS) * qkg_ref[...]
    q_ref[...] = qkn[:, :DIFF_WIDTH].astype(BF16)
    k_ref[...] = qkn[:, DIFF_WIDTH:].astype(BF16)
    v = jnp.dot(h, win_ref[:, 2 * CONV_CH + 2 * DIFF_WIDTH:], preferred_element_type=F32)
    vt_ref[0, :, 0] = v.T.reshape(N_HEADS, 2 * HEAD_DIM, tm).astype(BF16)

    tail = ubuf[tm:tm + CONV_HALO, :]
    ubuf[0:CONV_HALO, :] = jnp.where(i % seq_tiles == 0, 0.0, tail)
    ubuf[CONV_HALO:, :] = u
    for b in range(1, SUBLANES):
        ush[b - 1] = ubuf[b:b + ush.shape[1], :]


def _const_spec(shape):
    nd = len(shape)
    return pl.BlockSpec(shape, lambda i: (0,) * nd, pipeline_mode=pl.Buffered(1))


def _ffn_proj(x, n1, wgu, wd, nm, win, gsum, qkg, cw, cb, cn, seq):
    t = x.shape[0]
    tm = ROW_TILE
    n = t // tm
    cur = lambda w: pl.BlockSpec((tm, w), lambda i: (jnp.minimum(i, n - 1), 0))
    prev = lambda w: pl.BlockSpec((tm, w), lambda i: (jnp.maximum(i - 1, 0), 0))
    consts = (n1, wgu, wd, nm, win, gsum, qkg, cw, cb, cn)
    seq_tiles = seq // tm

    def vt_index(i):
        ii = jnp.minimum(i, n - 1)
        return (ii // seq_tiles, 0, ii % seq_tiles, 0, 0)

    return pl.pallas_call(
        functools.partial(_ffn_proj_body, seq_tiles),
        grid=(n + 1,),
        in_specs=[cur(D_MODEL)] + [_const_spec(a.shape) for a in consts],
        out_specs=[cur(D_MODEL), prev(CONV_CH), cur(DIFF_WIDTH), cur(DIFF_WIDTH),
                   pl.BlockSpec((1, N_HEADS, 1, 2 * HEAD_DIM, tm), vt_index)],
        out_shape=[jax.ShapeDtypeStruct((t, D_MODEL), F32), jax.ShapeDtypeStruct((t, CONV_CH), BF16),
                   jax.ShapeDtypeStruct((t, DIFF_WIDTH), BF16), jax.ShapeDtypeStruct((t, DIFF_WIDTH), BF16),
                   jax.ShapeDtypeStruct((t // seq, N_HEADS, seq_tiles, 2 * HEAD_DIM, tm), BF16)],
        scratch_shapes=[pltpu.VMEM((tm, D_FF), BF16), pltpu.VMEM((tm + CONV_HALO, CONV_CH), F32),
                        pltpu.VMEM((SUBLANES - 1, tm + CONV_HALO - SUBLANES, CONV_CH), F32)],
        compiler_params=pltpu.CompilerParams(dimension_semantics=("arbitrary",),
                                             vmem_limit_bytes=VMEM_LIMIT),
        name="ffn_proj",
    )(x, *consts)


def _mix_ffn_body(x_ref, c_ref, att_ref, woc_ref, wod_ref, n2_ref, wgu_ref, wd_ref, o_ref, act_ref):
    mix = (jnp.dot(c_ref[...], woc_ref[...], preferred_element_type=F32)
           + jnp.dot(att_ref[...], wod_ref[...], preferred_element_type=F32))
    x1 = x_ref[...] + mix
    o_ref[...] = _ffn(x1, n2_ref, wgu_ref, wd_ref, act_ref)


def _mix_ffn(x, c, att, woc, wod, n2, wgu, wd):
    t = x.shape[0]
    tm = ROW_TILE
    row = lambda w: pl.BlockSpec((tm, w), lambda i: (i, 0))
    consts = (woc, wod, n2, wgu, wd)
    return pl.pallas_call(
        _mix_ffn_body,
        grid=(t // tm,),
        in_specs=[row(D_MODEL), row(CONV_CH), row(DIFF_WIDTH)] + [_const_spec(a.shape) for a in consts],
        out_specs=row(D_MODEL),
        out_shape=jax.ShapeDtypeStruct((t, D_MODEL), F32),
        scratch_shapes=[pltpu.VMEM((tm, D_FF), BF16)],
        compiler_params=pltpu.CompilerParams(dimension_semantics=("arbitrary",),
                                             vmem_limit_bytes=VMEM_LIMIT),
        name="mix_ffn",
    )(x, c, att, *consts)


def _attn_body(lam_init, online, lamv_ref, q_ref, k_ref, vt_ref, bias_ref, sg_ref, o_ref,
               qbd_ref, m_ref, l_ref, acc1_ref, acc2_ref):
    T = Q_TILE
    TK = ATT_TILE
    ratio = T // TK
    i = pl.program_id(2)

    qt = q_ref[0].astype(F32).T
    row = lax.broadcasted_iota(jnp.int32, qt.shape, 0)
    qbd_ref[:, :T] = jnp.where(row < HEAD_DIM, qt, 0.0).astype(BF16)
    qbd_ref[:, T:] = jnp.where(row >= HEAD_DIM, qt, 0.0).astype(BF16)
    m_ref[...] = jnp.full(m_ref.shape, NEG_BIG, F32)
    l_ref[...] = jnp.zeros(l_ref.shape, F32)
    acc1_ref[...] = jnp.zeros(acc1_ref.shape, F32)
    acc2_ref[...] = jnp.zeros(acc2_ref.shape, F32)

    H2 = T // 2

    def on_upper_half(v, fill):
        pad = jnp.full((1, H2), fill, F32)
        return jnp.concatenate([pad, v[:, :H2], pad, v[:, H2:]], axis=1)

    def block(j, nblk, first_bias=None, plain=0, half_last=False):
        n_main = nblk - 1 if half_last else nblk
        kb = k_ref[0, pl.ds(pl.multiple_of(j * TK, TK), nblk * TK), :]
        s = jnp.dot(kb[:n_main * TK], qbd_ref[...], preferred_element_type=F32)
        if first_bias is not None:
            nb = n_main - plain
            b = bias_ref[first_bias:first_bias + nb, 0].reshape(nb * TK, T)
            biased = s[plain * TK:] + jnp.concatenate([b, b], axis=1)
            s = biased if plain == 0 else jnp.concatenate([s[:plain * TK], biased], axis=0)
        vtb = jnp.concatenate([vt_ref[0, 0, j + t] for t in range(n_main)], axis=1)
        if half_last:
            q_half = jnp.concatenate([qbd_ref[:, H2:T], qbd_ref[:, T + H2:]], axis=1)
            b_last = bias_ref[first_bias + n_main - plain, 0][:, H2:]
            s_last = (jnp.dot(kb[n_main * TK:], q_half, preferred_element_type=F32)
                      + jnp.concatenate([b_last, b_last], axis=1))
            vt_last = vt_ref[0, 0, j + n_main]
        if not online:
            p = jnp.exp2(s)
            lsum = jnp.sum(p, axis=0, keepdims=True)
            pb = p.astype(BF16)
            acc1_ref[...] += jnp.dot(vtb, pb[:, :T], preferred_element_type=F32)
            acc2_ref[...] += jnp.dot(vtb, pb[:, T:], preferred_element_type=F32)
            if half_last:
                p_last = jnp.exp2(s_last)
                lsum = lsum + on_upper_half(jnp.sum(p_last, axis=0, keepdims=True), 0.0)
                pb_last = p_last.astype(BF16)
                acc1_ref[:, H2:] += jnp.dot(vt_last, pb_last[:, :H2], preferred_element_type=F32)
                acc2_ref[:, H2:] += jnp.dot(vt_last, pb_last[:, H2:], preferred_element_type=F32)
            l_ref[...] += lsum
            return
        m_old = m_ref[...]
        cmax = jnp.max(s, axis=0, keepdims=True)
        if half_last:
            cmax = jnp.maximum(cmax, on_upper_half(jnp.max(s_last, axis=0, keepdims=True), NEG_BIG))
        m_new = jnp.maximum(m_old, cmax)
        alpha = jnp.exp2(m_old - m_new)
        p = jnp.exp2(s - m_new)
        lsum = jnp.sum(p, axis=0, keepdims=True)
        pb = p.astype(BF16)
        acc1_ref[...] = alpha[:, :T] * acc1_ref[...] + jnp.dot(vtb, pb[:, :T], preferred_element_type=F32)
        acc2_ref[...] = alpha[:, T:] * acc2_ref[...] + jnp.dot(vtb, pb[:, T:], preferred_element_type=F32)
        if half_last:
            m_half = jnp.concatenate([m_new[:, H2:T], m_new[:, T + H2:]], axis=1)
            p_last = jnp.exp2(s_last - m_half)
            lsum = lsum + on_upper_half(jnp.sum(p_last, axis=0, keepdims=True), 0.0)
            pb_last = p_last.astype(BF16)
            acc1_ref[:, H2:] += jnp.dot(vt_last, pb_last[:, :H2], preferred_element_type=F32)
            acc2_ref[:, H2:] += jnp.dot(vt_last, pb_last[:, H2:], preferred_element_type=F32)
        l_ref[...] = alpha * l_ref[...] + lsum
        m_ref[...] = m_new

    n_far = jnp.maximum(ratio * i - 2, 0)

    def far_step(jj, carry):
        block(FAR_STEP * jj, FAR_STEP)
        return carry

    lax.fori_loop(0, n_far // FAR_STEP, far_step, 0)
    done = (n_far // FAR_STEP) * FAR_STEP
    size = FAR_STEP // 2
    while size >= 2:
        take = (n_far - done) >= size

        @pl.when(take)
        def _(done=done, size=size):
            block(done, size)

        done = done + jnp.where(take, size, 0)
        size //= 2

    @pl.when(i > 0)
    def _():
        block(ratio * i - 2, ratio + 2, 0, plain=1, half_last=True)

    @pl.when(i == 0)
    def _():
        block(0, ratio, 1, half_last=True)

    lv = lamv_ref[...]
    lam = (jnp.exp(jnp.sum(lv[0:1] * lv[1:2], axis=1, keepdims=True))
           - jnp.exp(jnp.sum(lv[2:3] * lv[3:4], axis=1, keepdims=True)) + lam_init)
    l = l_ref[...]
    d = acc1_ref[...] / l[:, :T] - lam * (acc2_ref[...] / l[:, T:])
    ms = jnp.mean(d * d, axis=0, keepdims=True)
    y = d * lax.rsqrt(ms + EPS)
    o_ref[0] = (y.T * sg_ref[...]).astype(o_ref.dtype)


def _diff_attn(online, lam_init, q, k, vt, bias, lamv, sg):
    b, s, _ = q.shape
    T = Q_TILE
    TK = ATT_TILE
    nq = s // T
    return pl.pallas_call(
        functools.partial(_attn_body, lam_init, online),
        grid=(b, N_HEADS, nq),
        in_specs=[
            pl.BlockSpec(lamv.shape, lambda bi, h, i: (0, 0)),
            pl.BlockSpec((1, T, 2 * HEAD_DIM), lambda bi, h, i: (bi, i, h)),
            pl.BlockSpec((1, s, 2 * HEAD_DIM), lambda bi, h, i: (bi, 0, h)),
            pl.BlockSpec((1, 1, s // TK, 2 * HEAD_DIM, TK), lambda bi, h, i: (bi, h, 0, 0, 0)),
            pl.BlockSpec((bias.shape[0], 1, TK, T), lambda bi, h, i: (0, h, 0, 0),
                         pipeline_mode=pl.Buffered(1)),
            pl.BlockSpec(sg.shape, lambda bi, h, i: (0, 0)),
        ],
        out_specs=pl.BlockSpec((1, T, 2 * HEAD_DIM), lambda bi, h, i: (bi, i, h)),
        out_shape=jax.ShapeDtypeStruct((b, s, DIFF_WIDTH), BF16),
        scratch_shapes=[pltpu.VMEM((2 * HEAD_DIM, 2 * T), BF16), pltpu.VMEM((1, 2 * T), F32),
                        pltpu.VMEM((1, 2 * T), F32), pltpu.VMEM((2 * HEAD_DIM, T), F32),
                        pltpu.VMEM((2 * HEAD_DIM, T), F32)],
        compiler_params=pltpu.CompilerParams(dimension_semantics=("arbitrary", "arbitrary", "arbitrary"),
                                             vmem_limit_bytes=VMEM_LIMIT),
        name="diff_attn_online" if online else "diff_attn",
    )(lamv, q, k, vt, bias, sg)


def _rel_bucket_np(n):
    max_exact = NUM_BUCKETS // 2
    nf = np.maximum(n, 1).astype(np.float32)
    large = max_exact + (np.log(nf / max_exact) / math.log(MAX_DISTANCE / max_exact)
                         * (NUM_BUCKETS - max_exact)).astype(np.int32)
    large = np.minimum(large, NUM_BUCKETS - 1)
    return np.where(n < max_exact, n, large)


def _bias_body(tab_ref, idx_ref, o_ref):
    h = pl.program_id(1)
    idx = idx_ref[0]
    acc = jnp.zeros(idx.shape, F32)
    for b in range(NUM_BUCKETS):
        acc = jnp.where(idx == b, tab_ref[b, h], acc)
    o_ref[0, 0] = jnp.where(idx < 0, NEG_BIG, acc)


def _bias_tiles(tab):
    T, TK = Q_TILE, ATT_TILE
    n_tiles = T // TK + 1
    c = np.arange(TK)[:, None]
    r = np.arange(T)[None, :]
    dist = [r - c + TK * (1 - m) for m in range(n_tiles)]
    idx = jnp.asarray(np.stack([np.where(d >= 0, _rel_bucket_np(np.maximum(d, 0)), -1)
                                for d in dist]).astype(np.int32))
    return pl.pallas_call(
        _bias_body,
        grid=(n_tiles, N_HEADS),
        in_specs=[pl.BlockSpec(memory_space=pltpu.SMEM),
                  pl.BlockSpec((1, TK, T), lambda t, h: (t, 0, 0))],
        out_specs=pl.BlockSpec((1, 1, TK, T), lambda t, h: (t, h, 0, 0)),
        out_shape=jax.ShapeDtypeStruct((n_tiles, N_HEADS, TK, T), F32),
        name="bias_tiles",
    )(tab, idx)


def kernel(x, rel_bias, ffn1_norm, ffn1_w_gate, ffn1_w_up, ffn1_w_down, mix_norm, w_in, conv_w, conv_b,
           conv_norm, q_norm, k_norm, lambda_q1, lambda_k1, lambda_q2, lambda_k2, subln_norm, w_out,
           ffn2_norm, ffn2_w_gate, ffn2_w_up, ffn2_w_down):
    B, S, _ = x.shape
    depth = w_in.shape[0]
    assert S % Q_TILE == 0 and Q_TILE == 2 * ATT_TILE and ROW_TILE == ATT_TILE and ATT_TILE >= MAX_DISTANCE

    def pack_gu(wg, wu):
        return jnp.concatenate([wg.astype(BF16), wu.astype(BF16)], axis=-1)

    grp = np.arange(GROUP_MM) // HEAD_DIM
    gsum = jnp.asarray(grp[:, None] == grp[None, :], BF16)
    tab = (rel_bias - rel_bias[NUM_BUCKETS - 1:]).astype(F32) * LOG2E
    bias = _bias_tiles(tab)
    row2 = lambda a: a.reshape(1, -1).astype(F32)

    xf = x.reshape(B * S, D_MODEL)
    for l in range(depth):
        lam_init = 0.8 - 0.6 * math.exp(-0.3 * l)
        qg = q_norm[l].astype(F32) * (LOG2E / math.sqrt(HEAD_DIM))
        kg = k_norm[l].astype(F32)
        qkg = jnp.concatenate([jnp.tile(qg, 2 * N_HEADS), jnp.tile(kg, 2 * N_HEADS)]).reshape(1, -1)
        score_bound = (HEAD_DIM * 1.02) * jnp.max(jnp.abs(qg)) * jnp.max(jnp.abs(kg)) + jnp.max(jnp.abs(tab))
        cw8 = jnp.broadcast_to(conv_w[l].astype(F32)[:, None, :], (CONV_K, SUBLANES, CONV_CH))
        x1, c, q, k, vt = _ffn_proj(xf, row2(ffn1_norm[l]), pack_gu(ffn1_w_gate[l], ffn1_w_up[l]),
                                   ffn1_w_down[l].astype(BF16), row2(mix_norm[l]), w_in[l].astype(BF16),
                                   gsum, qkg, cw8, row2(conv_b[l]), row2(conv_norm[l]), S)
        lamv = jnp.stack([lambda_q1[l], lambda_k1[l], lambda_q2[l], lambda_k2[l]]).astype(F32)
        sg = row2(subln_norm[l]) * (1.0 - lam_init)
        att = lax.cond(score_bound <= SAFE_EXP2_BOUND,
                       functools.partial(_diff_attn, False, lam_init),
                       functools.partial(_diff_attn, True, lam_init),
                       q.reshape(B, S, DIFF_WIDTH), k.reshape(B, S, DIFF_WIDTH), vt, bias, lamv, sg)
        xf = _mix_ffn(x1, c, att.reshape(B * S, DIFF_WIDTH), w_out[l, :CONV_CH].astype(BF16),
                      w_out[l, CONV_CH:].astype(BF16), row2(ffn2_norm[l]),
                      pack_gu(ffn2_w_gate[l], ffn2_w_up[l]), ffn2_w_down[l].astype(BF16))
    return xf.reshape(B, S, D_MODEL)
```

```python
import functools
import math

import jax
import jax.numpy as jnp
import numpy as np
from jax import lax
from jax.experimental import pallas as pl
from jax.experimental.pallas import tpu as pltpu

F32 = jnp.float32
BF16 = jnp.bfloat16

D_MODEL = 1024
CONV_CH = 512
CONV_K = 31
DIFF_WIDTH = 512
HEAD_DIM = 64
N_HEADS = 4
D_FF = 2816
NUM_BUCKETS = 32
MAX_DISTANCE = 128
EPS = 1e-6
LOG2E = 1.4426950408889634
NEG_BIG = -1e30
SAFE_EXP2_BOUND = 90.0

FF_CHUNK = 256
N_FF_CHUNKS = D_FF // FF_CHUNK
ROW_TILE = 512
ATT_TILE = 512
Q_TILE = 1024
FAR_STEP = 4
SIDE_WORK_LAG = 2
LANES = 128
SUBLANES = 8
CONV_HALO = 32
CONV_ROWS = 24
GROUP_MM = 256
VMEM_LIMIT = 56 * 1024 * 1024


def _rms(x, g):
    ms = jnp.mean(x * x, axis=-1, keepdims=True)
    return x * lax.rsqrt(ms + EPS) * g


def _zero_after(v):
    bits = pltpu.bitcast(v[0:1, :], jnp.uint32)
    return pltpu.bitcast((bits >> 16) >> 16, F32)


def _ffn(x, g_ref, wg_ref, wu_ref, wd_ref, act_ref, side_work=None):
    h = _rms(x, g_ref[...]).astype(BF16)
    zs = [None] * SIDE_WORK_LAG
    for c in range(N_FF_CHUNKS):
        lo = c * FF_CHUNK
        gate = jnp.dot(h, wg_ref[:, lo:lo + FF_CHUNK], preferred_element_type=F32)
        up = jnp.dot(h, wu_ref[:, lo:lo + FF_CHUNK], preferred_element_type=F32)
        z = zs.pop(0)
        if z is not None:
            up = up + z
        act_ref[:, c * FF_CHUNK:(c + 1) * FF_CHUNK] = (gate * jax.nn.sigmoid(gate) * up).astype(BF16)
        zs.append(side_work(c, gate) if side_work is not None else None)
    y = jnp.dot(act_ref[...], wd_ref[...], preferred_element_type=F32)
    return x + 0.5 * y


def _conv_prev_tile(ubuf, ush, cw_ref, cb_ref, cn_ref, c_ref):
    tm = c_ref.shape[0]
    base = CONV_HALO - (CONV_K - 1)
    n_pieces = N_FF_CHUNKS
    rows_per = -(-tm // (n_pieces * SUBLANES)) * SUBLANES

    def rows_after(r0, nr, z):
        groups = nr // SUBLANES
        bias = cb_ref[...] + z
        acc = jnp.broadcast_to(bias, (groups, SUBLANES, CONV_CH))
        for j in range(CONV_K):
            a, b = divmod(base + j, SUBLANES)
            lo = r0 + a * SUBLANES
            rows = ubuf[lo:lo + nr, :] if b == 0 else ush[b - 1, lo:lo + nr, :]
            acc = acc + cw_ref[j][None] * rows.reshape(groups, SUBLANES, CONV_CH)
        c = _rms(acc.reshape(nr, CONV_CH), cn_ref[...])
        c = c * jax.nn.sigmoid(c)
        c_ref[r0:r0 + nr, :] = c.astype(c_ref.dtype)
        return _zero_after(jnp.sum(c, axis=0, keepdims=True))

    def piece(p, gate):
        zg = _zero_after(gate)
        z = jnp.concatenate([zg] * (CONV_CH // FF_CHUNK), axis=1)
        r0 = p * rows_per
        end = min(r0 + rows_per, tm)
        while r0 < end:
            nr = min(CONV_ROWS, end - r0)
            z = rows_after(r0, nr, z)
            r0 += nr
        return z[:, :FF_CHUNK]

    return piece


def _ffn_proj_body(seq_tiles, x_ref, n1_ref, wg_ref, wu_ref, wd_ref, nm_ref, win_ref, gsum_ref, qkg_ref,
                   cw_ref, cb_ref, cn_ref, xo_ref, c_ref, q_ref, k_ref, vt_ref, act_ref, ubuf, ush):
    i = pl.program_id(0)
    tm = x_ref.shape[0]

    @pl.when(i == 0)
    def _():
        ubuf[...] = jnp.zeros(ubuf.shape, F32)
        ush[...] = jnp.zeros(ush.shape, F32)

    conv_piece = _conv_prev_tile(ubuf, ush, cw_ref, cb_ref, cn_ref, c_ref)
    x1 = _ffn(x_ref[...], n1_ref, wg_ref, wu_ref, wd_ref, act_ref, side_work=conv_piece)
    xo_ref[...] = x1
    h = _rms(x1, nm_ref[...]).astype(BF16)
    ag = jnp.dot(h, win_ref[:, 0:2 * CONV_CH], preferred_element_type=F32)
    u = ag[:, :CONV_CH] * jax.nn.sigmoid(ag[:, CONV_CH:])
    qk =jnp.dot(h, win_ref[:, 2 * CONV_CH:2 * CONV_CH + 2 * DIFF_WIDTH], preferred_element_type=F32)
    sq = (qk * qk).astype(BF16)
    ss = jnp.concatenate(
        [jnp.dot(sq[:, c * GROUP_MM:(c + 1) * GROUP_MM], gsum_ref[...], preferred_element_type=F32)
         for c in range(2 * DIFF_WIDTH // GROUP_MM)], axis=1)
    qkn = qk * lax.rsqrt(ss * (1.0 / HEAD_DIM) + EPS) * qkg_ref[...]
    q_ref[...] = qkn[:, :DIFF_WIDTH].astype(BF16)
    k_ref[...] = qkn[:, DIFF_WIDTH:].astype(BF16)
    v = jnp.dot(h, win_ref[:, 2 * CONV_CH + 2 * DIFF_WIDTH:], preferred_element_type=F32)
    vt_ref[0, :, 0] = v.T.reshape(N_HEADS, 2 * HEAD_DIM, tm).astype(BF16)

    tail = ubuf[tm:tm + CONV_HALO, :]
    ubuf[0:CONV_HALO, :] = jnp.where(i % seq_tiles == 0, 0.0, tail)
    ubuf[CONV_HALO:, :] = u
    for b in range(1, SUBLANES):
        ush[b - 1] = ubuf[b:b + ush.shape[1], :]


def _const_spec(shape):
    nd = len(shape)
    return pl.BlockSpec(shape, lambda i: (0,) * nd, pipeline_mode=pl.Buffered(1))


def _ffn_proj(x, n1, wg, wu, wd, nm, win, gsum, qkg, cw, cb, cn, seq):
    t = x.shape[0]
    tm = ROW_TILE
    n = t // tm
    cur = lambda w: pl.BlockSpec((tm, w), lambda i: (jnp.minimum(i, n - 1), 0))
    prev = lambda w: pl.BlockSpec((tm, w), lambda i: (jnp.maximum(i - 1, 0), 0))
    consts = (n1, wg, wu, wd, nm, win, gsum, qkg, cw, cb, cn)
    seq_tiles = seq // tm

    def vt_index(i):
        ii = jnp.minimum(i, n - 1)
        return (ii // seq_tiles, 0, ii % seq_tiles, 0, 0)

    return pl.pallas_call(
        functools.partial(_ffn_proj_body, seq_tiles),
        grid=(n + 1,),
        in_specs=[cur(D_MODEL)] + [_const_spec(a.shape) for a in consts],
        out_specs=[cur(D_MODEL), prev(CONV_CH), cur(DIFF_WIDTH), cur(DIFF_WIDTH),
                   pl.BlockSpec((1, N_HEADS, 1, 2 * HEAD_DIM, tm), vt_index)],
        out_shape=[jax.ShapeDtypeStruct((t, D_MODEL), F32), jax.ShapeDtypeStruct((t, CONV_CH), BF16),
                   jax.ShapeDtypeStruct((t, DIFF_WIDTH), BF16), jax.ShapeDtypeStruct((t, DIFF_WIDTH), BF16),
                   jax.ShapeDtypeStruct((t // seq, N_HEADS, seq_tiles, 2 * HEAD_DIM, tm), BF16)],
        scratch_shapes=[pltpu.VMEM((tm, D_FF), BF16), pltpu.VMEM((tm + CONV_HALO, CONV_CH), F32),
                        pltpu.VMEM((SUBLANES - 1, tm + CONV_HALO - SUBLANES, CONV_CH), F32)],
        compiler_params=pltpu.CompilerParams(dimension_semantics=("arbitrary",),
                                             vmem_limit_bytes=VMEM_LIMIT),
        name="ffn_proj",
    )(x, *consts)


def _mix_ffn_body(x_ref, c_ref, att_ref, woc_ref, wod_ref, n2_ref, wg_ref, wu_ref, wd_ref, o_ref,
                  act_ref):
    mix = (jnp.dot(c_ref[...], woc_ref[...], preferred_element_type=F32)
           + jnp.dot(att_ref[...], wod_ref[...], preferred_element_type=F32))
    x1 = x_ref[...] + mix
    o_ref[...] = _ffn(x1, n2_ref, wg_ref, wu_ref, wd_ref, act_ref)


def _mix_ffn(x, c, att, woc, wod, n2, wg, wu, wd):
    t = x.shape[0]
    tm = ROW_TILE
    row = lambda w: pl.BlockSpec((tm, w), lambda i: (i, 0))
    consts = (woc, wod, n2, wg, wu, wd)
    return pl.pallas_call(
        _mix_ffn_body,
        grid=(t // tm,),
        in_specs=[row(D_MODEL), row(CONV_CH), row(DIFF_WIDTH)] + [_const_spec(a.shape) for a in consts],
        out_specs=row(D_MODEL),
        out_shape=jax.ShapeDtypeStruct((t, D_MODEL), F32),
        scratch_shapes=[pltpu.VMEM((tm, D_FF), BF16)],
        compiler_params=pltpu.CompilerParams(dimension_semantics=("arbitrary",),
                                             vmem_limit_bytes=VMEM_LIMIT),
        name="mix_ffn",
    )(x, c, att, *consts)


def _attn_body(lam_init, online, lamv_ref, q_ref, k_ref, vt_ref, bias_ref, sg_ref, o_ref,
               qbd_ref, m_ref, l_ref, acc1_ref, acc2_ref):
    T = Q_TILE
    TK = ATT_TILE
    ratio = T // TK
    i = pl.program_id(2)

    qt = q_ref[0].astype(F32).T
    row = lax.broadcasted_iota(jnp.int32, qt.shape, 0)
    qbd_ref[:, :T] = jnp.where(row < HEAD_DIM, qt, 0.0).astype(BF16)
    qbd_ref[:, T:] = jnp.where(row >= HEAD_DIM, qt, 0.0).astype(BF16)
    if online:
        m_ref[...] = jnp.full(m_ref.shape, NEG_BIG, F32)
        l_ref[...] = jnp.zeros(l_ref.shape, F32)
        acc1_ref[...] = jnp.zeros(acc1_ref.shape, F32)
        acc2_ref[...] = jnp.zeros(acc2_ref.shape, F32)

    H2 = T // 2

    def on_upper_half(v, fill):
        pad = jnp.full((1, H2), fill, F32)
        return jnp.concatenate([pad, v[:, :H2], pad, v[:, H2:]], axis=1)

    def block(j, nblk, first_bias=None, plain=0, half_last=False, first=False):
        n_main = nblk - 1 if half_last else nblk
        kb = k_ref[0, pl.ds(pl.multiple_of(j * TK, TK), nblk * TK), :]
        s = jnp.dot(kb[:n_main * TK], qbd_ref[...], preferred_element_type=F32)
        if first_bias is not None:
            nb = n_main - plain
            b = bias_ref[first_bias:first_bias + nb, 0].reshape(nb * TK, T)
            biased = s[plain * TK:] + jnp.concatenate([b, b], axis=1)
            s = biased if plain == 0 else jnp.concatenate([s[:plain * TK], biased], axis=0)
        vtb = jnp.concatenate([vt_ref[0, 0, j + t] for t in range(n_main)], axis=1)
        if half_last:
            q_half = jnp.concatenate([qbd_ref[:, H2:T], qbd_ref[:, T + H2:]], axis=1)
            b_last = bias_ref[first_bias + n_main - plain, 0][:, H2:]
            s_last = (jnp.dot(kb[n_main * TK:], q_half, preferred_element_type=F32)
                      + jnp.concatenate([b_last, b_last], axis=1))
            vt_last = vt_ref[0, 0, j + n_main]
        if not online:
            p = jnp.exp2(s)
            lsum = jnp.sum(p, axis=0, keepdims=True)
            pb = p.astype(BF16)
            pv1 = jnp.dot(vtb, pb[:, :T], preferred_element_type=F32)
            pv2 = jnp.dot(vtb, pb[:, T:], preferred_element_type=F32)
            if half_last:
                p_last = jnp.exp2(s_last)
                lsum = lsum + on_upper_half(jnp.sum(p_last, axis=0, keepdims=True), 0.0)
                pb_last = p_last.astype(BF16)
            acc1_ref[...] = pv1 if first else acc1_ref[...] + pv1
            acc2_ref[...] = pv2 if first else acc2_ref[...] + pv2
            l_ref[...] = lsum if first else l_ref[...] + lsum
            if half_last:
                acc1_ref[:, H2:] += jnp.dot(vt_last, pb_last[:, :H2], preferred_element_type=F32)
                acc2_ref[:, H2:] += jnp.dot(vt_last, pb_last[:, H2:], preferred_element_type=F32)
            return
        m_old = m_ref[...]
        cmax = jnp.max(s, axis=0, keepdims=True)
        if half_last:
            cmax = jnp.maximum(cmax, on_upper_half(jnp.max(s_last, axis=0, keepdims=True), NEG_BIG))
        m_new = jnp.maximum(m_old, cmax)
        alpha = jnp.exp2(m_old - m_new)
        p = jnp.exp2(s - m_new)
        lsum = jnp.sum(p, axis=0, keepdims=True)
        pb = p.astype(BF16)
        acc1_ref[...] = alpha[:, :T] * acc1_ref[...] + jnp.dot(vtb, pb[:, :T], preferred_element_type=F32)
        acc2_ref[...] = alpha[:, T:] * acc2_ref[...] + jnp.dot(vtb, pb[:, T:], preferred_element_type=F32)
        if half_last:
            m_half = jnp.concatenate([m_new[:, H2:T], m_new[:, T + H2:]], axis=1)
            p_last = jnp.exp2(s_last - m_half)
            lsum = lsum + on_upper_half(jnp.sum(p_last, axis=0, keepdims=True), 0.0)
            pb_last = p_last.astype(BF16)
            acc1_ref[:, H2:] += jnp.dot(vt_last, pb_last[:, :H2], preferred_element_type=F32)
            acc2_ref[:, H2:] += jnp.dot(vt_last, pb_last[:, H2:], preferred_element_type=F32)
        l_ref[...] = alpha * l_ref[...] + lsum
        m_ref[...] = m_new

    @pl.when(i > 0)
    def _():
        block(ratio * i - 2, ratio + 2, 0, plain=1, half_last=True, first=True)

    @pl.when(i == 0)
    def _():
        block(0, ratio, 1, half_last=True, first=True)

    n_far = jnp.maximum(ratio * i - 2, 0)

    def far_step(jj, carry):
        block(FAR_STEP * jj, FAR_STEP)
        return carry

    lax.fori_loop(0, n_far // FAR_STEP, far_step, 0)
    done = (n_far // FAR_STEP) * FAR_STEP
    size = FAR_STEP // 2
    while size >= 2:
        take = (n_far - done) >= size

        @pl.when(take)
        def _(done=done, size=size):
            block(done, size)

        done = done + jnp.where(take, size, 0)
        size //= 2

    lv = lamv_ref[...]
    lam = (jnp.exp(jnp.sum(lv[0:1] * lv[1:2], axis=1, keepdims=True))
           - jnp.exp(jnp.sum(lv[2:3] * lv[3:4], axis=1, keepdims=True)) + lam_init)
    l = l_ref[...]
    d = acc1_ref[...] / l[:, :T] - lam * (acc2_ref[...] / l[:, T:])
    ms = jnp.mean(d * d, axis=0, keepdims=True)
    y = d * lax.rsqrt(ms + EPS)
    o_ref[0] = (y.T * sg_ref[...]).astype(o_ref.dtype)


def _diff_attn(online, lam_init, q, k, vt, bias, lamv, sg):
    b, s, _ = q.shape
    T = Q_TILE
    TK = ATT_TILE
    nq = s // T
    return pl.pallas_call(
        functools.partial(_attn_body, lam_init, online),
        grid=(b, N_HEADS, nq),
        in_specs=[
            pl.BlockSpec(lamv.shape, lambda bi, h, i: (0, 0)),
            pl.BlockSpec((1, T, 2 * HEAD_DIM), lambda bi, h, i: (bi, i, h)),
            pl.BlockSpec((1, s, 2 * HEAD_DIM), lambda bi, h, i: (bi, 0, h)),
            pl.BlockSpec((1, 1, s // TK, 2 * HEAD_DIM, TK), lambda bi, h, i: (bi, h, 0, 0, 0)),
            pl.BlockSpec((bias.shape[0], 1, TK, T), lambda bi, h, i: (0, h, 0, 0),
                         pipeline_mode=pl.Buffered(1)),
            pl.BlockSpec(sg.shape, lambda bi, h, i: (0, 0)),
        ],
        out_specs=pl.BlockSpec((1, T, 2 * HEAD_DIM), lambda bi, h, i: (bi, i, h)),
        out_shape=jax.ShapeDtypeStruct((b, s, DIFF_WIDTH), BF16),
        scratch_shapes=[pltpu.VMEM((2 * HEAD_DIM, 2 * T), BF16), pltpu.VMEM((1, 2 * T), F32),
                        pltpu.VMEM((1, 2 * T), F32), pltpu.VMEM((2 * HEAD_DIM, T), F32),
                        pltpu.VMEM((2 * HEAD_DIM, T), F32)],
        compiler_params=pltpu.CompilerParams(dimension_semantics=("arbitrary", "arbitrary", "arbitrary"),
                                             vmem_limit_bytes=VMEM_LIMIT),
        name="diff_attn_online" if online else "diff_attn",
    )(lamv, q, k, vt, bias, sg)


def _rel_bucket_np(n):
    max_exact = NUM_BUCKETS // 2
    nf = np.maximum(n, 1).astype(np.float32)
    large = max_exact + (np.log(nf / max_exact) / math.log(MAX_DISTANCE / max_exact)
                         * (NUM_BUCKETS - max_exact)).astype(np.int32)
    large = np.minimum(large, NUM_BUCKETS - 1)
    return np.where(n < max_exact, n, large)


def _bias_body(kinds, tab_ref, idx_ref, o_ref):
    h = pl.program_id(0)
    for t, tile_kinds in enumerate(kinds):
        for i, row_kinds in enumerate(tile_kinds):
            for j, kind in enumerate(row_kinds):
                rows, cols = slice(i * LANES, (i + 1) * LANES), slice(j * LANES, (j + 1) * LANES)
                if kind != "band":
                    fill = NEG_BIG if kind == "masked" else 0.0
                    o_ref[t, 0, rows, cols] = jnp.full((LANES, LANES), fill, F32)
                    continue
                idx = idx_ref[t, rows, cols]
                acc = jnp.zeros(idx.shape, F32)
                for b in range(NUM_BUCKETS):
                    acc = jnp.where(idx == b, tab_ref[b, h], acc)
                o_ref[t, 0, rows, cols] = jnp.where(idx < 0, NEG_BIG, acc)


def _bias_tiles(tab):
    T, TK = Q_TILE, ATT_TILE
    n_tiles = T // TK + 1
    c = np.arange(TK)[:, None]
    r = np.arange(T)[None, :]
    dist = [r - c + TK * (1 - m) for m in range(n_tiles)]
    idx = jnp.asarray(np.stack([np.where(d >= 0, _rel_bucket_np(np.maximum(d, 0)), -1)
                                for d in dist]).astype(np.int32))

    def kind(d):
        return "masked" if (d < 0).all() else "zero" if (d >= MAX_DISTANCE).all() else "band"

    kinds = tuple(tuple(tuple(kind(d[i:i + LANES, j:j + LANES]) for j in range(0, T, LANES))
                        for i in range(0, TK, LANES)) for d in dist)
    return pl.pallas_call(
        functools.partial(_bias_body, kinds),
        grid=(N_HEADS,),
        in_specs=[pl.BlockSpec(memory_space=pltpu.SMEM),
                  pl.BlockSpec((n_tiles, TK, T), lambda h: (0, 0, 0))],
        out_specs=pl.BlockSpec((n_tiles, 1, TK, T), lambda h: (0, h, 0, 0)),
        out_shape=jax.ShapeDtypeStruct((n_tiles, N_HEADS, TK, T), F32),
        name="bias_tiles",
    )(tab, idx)


def kernel(x, rel_bias, ffn1_norm, ffn1_w_gate, ffn1_w_up, ffn1_w_down, mix_norm, w_in, conv_w, conv_b,
           conv_norm, q_norm, k_norm, lambda_q1, lambda_k1, lambda_q2, lambda_k2, subln_norm, w_out,
           ffn2_norm, ffn2_w_gate, ffn2_w_up, ffn2_w_down):
    B, S, _ = x.shape
    depth = w_in.shape[0]
    assert S % Q_TILE == 0 and Q_TILE == 2 * ATT_TILE and ROW_TILE == ATT_TILE and ATT_TILE >= MAX_DISTANCE

    grp = np.arange(GROUP_MM) // HEAD_DIM
    gsum = jnp.asarray(grp[:, None] == grp[None, :], BF16)
    tab = (rel_bias - rel_bias[NUM_BUCKETS - 1:]).astype(F32) * LOG2E
    bias = _bias_tiles(tab)
    row2 = lambda a: a.reshape(1, -1).astype(F32)

    xf = x.reshape(B * S, D_MODEL)
    for l in range(depth):
        lam_init = 0.8 - 0.6 * math.exp(-0.3 * l)
        qg = q_norm[l].astype(F32) * (LOG2E / math.sqrt(HEAD_DIM))
        kg = k_norm[l].astype(F32)
        qkg = jnp.concatenate([jnp.tile(qg, 2 * N_HEADS), jnp.tile(kg, 2 * N_HEADS)]).reshape(1, -1)
        score_bound = (HEAD_DIM * 1.02) * jnp.max(jnp.abs(qg)) * jnp.max(jnp.abs(kg)) + jnp.max(jnp.abs(tab))
        cw8 = jnp.broadcast_to(conv_w[l].astype(F32)[:, None, :], (CONV_K, SUBLANES, CONV_CH))
        x1, c, q, k, vt = _ffn_proj(xf, row2(ffn1_norm[l]), ffn1_w_gate[l].astype(BF16),
                                   ffn1_w_up[l].astype(BF16), ffn1_w_down[l].astype(BF16),
                                   row2(mix_norm[l]), w_in[l].astype(BF16), gsum, qkg, cw8,
                                   row2(conv_b[l]), row2(conv_norm[l]), S)
        lamv = jnp.stack([lambda_q1[l], lambda_k1[l], lambda_q2[l], lambda_k2[l]]).astype(F32)
        sg = row2(subln_norm[l]) * (1.0 - lam_init)
        att = lax.cond(score_bound <= SAFE_EXP2_BOUND,
                       functools.partial(_diff_attn, False, lam_init),
                       functools.partial(_diff_attn, True, lam_init),
                       q.reshape(B, S, DIFF_WIDTH), k.reshape(B, S, DIFF_WIDTH), vt, bias, lamv, sg)
        xf = _mix_ffn(x1, c, att.reshape(B * S, DIFF_WIDTH), w_out[l, :CONV_CH].astype(BF16),
                      w_out[l, CONV_CH:].astype(BF16), row2(ffn2_norm[l]), ffn2_w_gate[l].astype(BF16),
                      ffn2_w_up[l].astype(BF16), ffn2_w_down[l].astype(BF16))
    return xf.reshape(B, S, D_MODEL)
```

```python
import functools
import math

import jax
import jax.numpy as jnp
import numpy as np
from jax import lax
from jax.experimental import pallas as pl
from jax.experimental.pallas import tpu as pltpu

F32 = jnp.float32
BF16 = jnp.bfloat16

D_MODEL = 1024
CONV_CH = 512
CONV_K = 31
DIFF_WIDTH = 512
HEAD_DIM = 64
N_HEADS = 4
D_FF = 2816
NUM_BUCKETS = 32
MAX_DISTANCE = 128
EPS = 1e-6
LOG2E = 1.4426950408889634
NEG_BIG = -1e30
SAFE_EXP2_BOUND = 90.0

FF_CHUNK = 256
N_FF_CHUNKS = D_FF // FF_CHUNK
ROW_TILE = 512
MIX_TILE = 1024
ATT_TILE = 512
Q_TILE = 1024
FAR_STEP = 4
SIDE_WORK_LAG = 2
LANES = 128
SUBLANES = 8
CONV_HALO = 32
CONV_ROWS = 24
GROUP_MM = 256
VMEM_LIMIT = 56 * 1024 * 1024


def _rms(x, g):
    ms = jnp.mean(x * x, axis=-1, keepdims=True)
    return x * lax.rsqrt(ms + EPS) * g


def _zero_after(v):
    bits = pltpu.bitcast(v[0:1, :], jnp.uint32)
    return pltpu.bitcast((bits >> 16) >> 16, F32)


def _ffn(x, g_ref, wg_ref, wu_ref, wd_ref, act_ref, side_work=None):
    h = _rms(x, g_ref[...]).astype(BF16)
    zs = [None] * SIDE_WORK_LAG
    for c in range(N_FF_CHUNKS):
        lo = c * FF_CHUNK
        gate = jnp.dot(h, wg_ref[:, lo:lo + FF_CHUNK], preferred_element_type=F32)
        up = jnp.dot(h, wu_ref[:, lo:lo + FF_CHUNK], preferred_element_type=F32)
        z = zs.pop(0)
        if z is not None:
            up = up + z
        act_ref[:, c * FF_CHUNK:(c + 1) * FF_CHUNK] = (gate * jax.nn.sigmoid(gate) * up).astype(BF16)
        zs.append(side_work(c, gate) if side_work is not None else None)
    y = jnp.dot(act_ref[...], wd_ref[...], preferred_element_type=F32)
    return x + 0.5 * y


def _conv_prev_tile(ubuf, ush, cw_ref, cb_ref, cn_ref, c_ref):
    tm = c_ref.shape[0]
    base = CONV_HALO - (CONV_K - 1)
    n_pieces = N_FF_CHUNKS
    rows_per = -(-tm // (n_pieces * SUBLANES)) * SUBLANES

    def rows_after(r0, nr, z):
        groups = nr // SUBLANES
        bias = cb_ref[...] + z
        acc = jnp.broadcast_to(bias, (groups, SUBLANES, CONV_CH))
        for j in range(CONV_K):
            a, b = divmod(base + j, SUBLANES)
            lo = r0 + a * SUBLANES
            rows = ubuf[lo:lo + nr, :] if b == 0 else ush[b - 1, lo:lo + nr, :]
            acc = acc + cw_ref[j][None] * rows.reshape(groups, SUBLANES, CONV_CH)
        c = _rms(acc.reshape(nr, CONV_CH), cn_ref[...])
        c = c * jax.nn.sigmoid(c)
        c_ref[r0:r0 + nr, :] = c.astype(c_ref.dtype)
        return _zero_after(jnp.sum(c, axis=0, keepdims=True))

    def piece(p, gate):
        zg = _zero_after(gate)
        z = jnp.concatenate([zg] * (CONV_CH // FF_CHUNK), axis=1)
        r0 = p * rows_per
        end = min(r0 + rows_per, tm)
        while r0 < end:
            nr = min(CONV_ROWS, end - r0)
            z = rows_after(r0, nr, z)
            r0 += nr
        return z[:, :FF_CHUNK]

    return piece


def _ffn_proj_body(seq_tiles, x_ref, n1_ref, wg_ref, wu_ref, wd_ref, nm_ref, win_ref, gsum_ref, qkg_ref,
                   cw_ref, cb_ref, cn_ref, xo_ref, c_ref, q_ref, k_ref, vt_ref, act_ref, ubuf, ush):
    i = pl.program_id(0)
    tm = x_ref.shape[0]

    @pl.when(i == 0)
    def _():
        ubuf[...] = jnp.zeros(ubuf.shape, F32)
        ush[...] = jnp.zeros(ush.shape, F32)

    conv_piece = _conv_prev_tile(ubuf, ush, cw_ref, cb_ref, cn_ref, c_ref)
    x1 = _ffn(x_ref[...], n1_ref, wg_ref, wu_ref, wd_ref, act_ref, side_work=conv_piece)
    xo_ref[...] = x1
    h = _rms(x1, nm_ref[...]).astype(BF16)
    ag = jnp.dot(h, win_ref[:, 0:2 * CONV_CH], preferred_element_type=F32)
    u = ag[:, :CONV_CH] * jax.nn.sigmoid(ag[:, CONV_CH:])
    qk =jnp.dot(h, win_ref[:, 2 * CONV_CH:2 * CONV_CH + 2 * DIFF_WIDTH], preferred_element_type=F32)
    sq = (qk * qk).astype(BF16)
    ss = jnp.concatenate(
        [jnp.dot(sq[:, c * GROUP_MM:(c + 1) * GROUP_MM], gsum_ref[...], preferred_element_type=F32)
         for c in range(2 * DIFF_WIDTH // GROUP_MM)], axis=1)
    qkn = qk * lax.rsqrt(ss * (1.0 / HEAD_DIM) + EPS) * qkg_ref[...]
    q_ref[...] = qkn[:, :DIFF_WIDTH].astype(BF16)
    k_ref[...] = qkn[:, DIFF_WIDTH:].astype(BF16)
    v = jnp.dot(h, win_ref[:, 2 * CONV_CH + 2 * DIFF_WIDTH:], preferred_element_type=F32)
    vt_ref[0, :, 0] = v.T.reshape(N_HEADS, 2 * HEAD_DIM, tm).astype(BF16)

    tail = ubuf[tm:tm + CONV_HALO, :]
    ubuf[0:CONV_HALO, :] = jnp.where(i % seq_tiles == 0, 0.0, tail)
    ubuf[CONV_HALO:, :] = u
    for b in range(1, SUBLANES):
        ush[b - 1] = ubuf[b:b + ush.shape[1], :]


def _const_spec(shape):
    nd = len(shape)
    return pl.BlockSpec(shape, lambda i: (0,) * nd, pipeline_mode=pl.Buffered(1))


def _ffn_proj(x, n1, wg, wu, wd, nm, win, gsum, qkg, cw, cb, cn, seq):
    t = x.shape[0]
    tm = ROW_TILE
    n = t // tm
    cur = lambda w: pl.BlockSpec((tm, w), lambda i: (jnp.minimum(i, n - 1), 0))
    prev = lambda w: pl.BlockSpec((tm, w), lambda i: (jnp.maximum(i - 1, 0), 0))
    consts = (n1, wg, wu, wd, nm, win, gsum, qkg, cw, cb, cn)
    seq_tiles = seq // tm

    def vt_index(i):
        ii = jnp.minimum(i, n - 1)
        return (ii // seq_tiles, 0, ii % seq_tiles, 0, 0)

    return pl.pallas_call(
        functools.partial(_ffn_proj_body, seq_tiles),
        grid=(n + 1,),
        in_specs=[cur(D_MODEL)] + [_const_spec(a.shape) for a in consts],
        out_specs=[cur(D_MODEL), prev(CONV_CH), cur(DIFF_WIDTH), cur(DIFF_WIDTH),
                   pl.BlockSpec((1, N_HEADS, 1, 2 * HEAD_DIM, tm), vt_index)],
        out_shape=[jax.ShapeDtypeStruct((t, D_MODEL), F32), jax.ShapeDtypeStruct((t, CONV_CH), BF16),
                   jax.ShapeDtypeStruct((t, DIFF_WIDTH), BF16), jax.ShapeDtypeStruct((t, DIFF_WIDTH), BF16),
                   jax.ShapeDtypeStruct((t // seq, N_HEADS, seq_tiles, 2 * HEAD_DIM, tm), BF16)],
        scratch_shapes=[pltpu.VMEM((tm, D_FF), BF16), pltpu.VMEM((tm + CONV_HALO, CONV_CH), F32),
                        pltpu.VMEM((SUBLANES - 1, tm + CONV_HALO - SUBLANES, CONV_CH), F32)],
        compiler_params=pltpu.CompilerParams(dimension_semantics=("arbitrary",),
                                             vmem_limit_bytes=VMEM_LIMIT),
        name="ffn_proj",
    )(x, *consts)


def _mix_ffn_body(x_ref, c_ref, att_ref, woc_ref, wod_ref, n2_ref, wg_ref, wu_ref, wd_ref, o_ref,
                  act_ref):
    mix = (jnp.dot(c_ref[...], woc_ref[...], preferred_element_type=F32)
           + jnp.dot(att_ref[...], wod_ref[...], preferred_element_type=F32))
    x1 = x_ref[...] + mix
    o_ref[...] = _ffn(x1, n2_ref, wg_ref, wu_ref, wd_ref, act_ref)


def _mix_ffn(x, c, att, woc, wod, n2, wg, wu, wd):
    t = x.shape[0]
    tm = MIX_TILE
    row = lambda w: pl.BlockSpec((tm, w), lambda i: (i, 0))
    consts = (woc, wod, n2, wg, wu, wd)
    return pl.pallas_call(
        _mix_ffn_body,
        grid=(t // tm,),
        in_specs=[row(D_MODEL), row(CONV_CH), row(DIFF_WIDTH)] + [_const_spec(a.shape) for a in consts],
        out_specs=row(D_MODEL),
        out_shape=jax.ShapeDtypeStruct((t, D_MODEL), F32),
        scratch_shapes=[pltpu.VMEM((tm, D_FF), BF16)],
        compiler_params=pltpu.CompilerParams(dimension_semantics=("arbitrary",),
                                             vmem_limit_bytes=VMEM_LIMIT),
        name="mix_ffn",
    )(x, c, att, *consts)


def _attn_body(lam_init, online, lamv_ref, q_ref, k_ref, vt_ref, bias_ref, sg_ref, o_ref,
               qbd_ref, m_ref, l_ref, acc1_ref, acc2_ref):
    T = Q_TILE
    TK = ATT_TILE
    ratio = T // TK
    i = pl.program_id(2)

    qt = q_ref[0].astype(F32).T
    row = lax.broadcasted_iota(jnp.int32, qt.shape, 0)
    qbd_ref[:, :T] = jnp.where(row < HEAD_DIM, qt, 0.0).astype(BF16)
    qbd_ref[:, T:] = jnp.where(row >= HEAD_DIM, qt, 0.0).astype(BF16)
    if online:
        m_ref[...] = jnp.full(m_ref.shape, NEG_BIG, F32)
        l_ref[...] = jnp.zeros(l_ref.shape, F32)
        acc1_ref[...] = jnp.zeros(acc1_ref.shape, F32)
        acc2_ref[...] = jnp.zeros(acc2_ref.shape, F32)

    H2 = T // 2

    def on_upper_half(v, fill):
        pad = jnp.full((1, H2), fill, F32)
        return jnp.concatenate([pad, v[:, :H2], pad, v[:, H2:]], axis=1)

    def block(j, nblk, first_bias=None, plain=0, half_last=False, first=False):
        n_main = nblk - 1 if half_last else nblk
        kb = k_ref[0, pl.ds(pl.multiple_of(j * TK, TK), nblk * TK), :]
        s = jnp.dot(kb[:n_main * TK], qbd_ref[...], preferred_element_type=F32)
        if first_bias is not None:
            nb = n_main - plain
            b = bias_ref[first_bias:first_bias + nb, 0].reshape(nb * TK, T)
            biased = s[plain * TK:] + jnp.concatenate([b, b], axis=1)
            s = biased if plain == 0 else jnp.concatenate([s[:plain * TK], biased], axis=0)
        vtb = jnp.concatenate([vt_ref[0, 0, j + t] for t in range(n_main)], axis=1)
        if half_last:
            q_half = jnp.concatenate([qbd_ref[:, H2:T], qbd_ref[:, T + H2:]], axis=1)
            b_last = bias_ref[first_bias + n_main - plain, 0][:, H2:]
            s_last = (jnp.dot(kb[n_main * TK:], q_half, preferred_element_type=F32)
                      + jnp.concatenate([b_last, b_last], axis=1))
            vt_last = vt_ref[0, 0, j + n_main]
        if not online:
            p = jnp.exp2(s)
            lsum = jnp.sum(p, axis=0, keepdims=True)
            pb = p.astype(BF16)
            pv1 = jnp.dot(vtb, pb[:, :T], preferred_element_type=F32)
            pv2 = jnp.dot(vtb, pb[:, T:], preferred_element_type=F32)
            if half_last:
                p_last = jnp.exp2(s_last)
                lsum = lsum + on_upper_half(jnp.sum(p_last, axis=0, keepdims=True), 0.0)
                pb_last = p_last.astype(BF16)
            acc1_ref[...] = pv1 if first else acc1_ref[...] + pv1
            acc2_ref[...] = pv2 if first else acc2_ref[...] + pv2
            l_ref[...] = lsum if first else l_ref[...] + lsum
            if half_last:
                acc1_ref[:, H2:] += jnp.dot(vt_last, pb_last[:, :H2], preferred_element_type=F32)
                acc2_ref[:, H2:] += jnp.dot(vt_last, pb_last[:, H2:], preferred_element_type=F32)
            return
        m_old = m_ref[...]
        cmax = jnp.max(s, axis=0, keepdims=True)
        if half_last:
            cmax = jnp.maximum(cmax, on_upper_half(jnp.max(s_last, axis=0, keepdims=True), NEG_BIG))
        m_new = jnp.maximum(m_old, cmax)
        alpha = jnp.exp2(m_old - m_new)
        p = jnp.exp2(s - m_new)
        lsum = jnp.sum(p, axis=0, keepdims=True)
        pb = p.astype(BF16)
        acc1_ref[...] = alpha[:, :T] * acc1_ref[...] + jnp.dot(vtb, pb[:, :T], preferred_element_type=F32)
        acc2_ref[...] = alpha[:, T:] * acc2_ref[...] + jnp.dot(vtb, pb[:, T:], preferred_element_type=F32)
        if half_last:
            m_half = jnp.concatenate([m_new[:, H2:T], m_new[:, T + H2:]], axis=1)
            p_last = jnp.exp2(s_last - m_half)
            lsum = lsum + on_upper_half(jnp.sum(p_last, axis=0, keepdims=True), 0.0)
            pb_last = p_last.astype(BF16)
            acc1_ref[:, H2:] += jnp.dot(vt_last, pb_last[:, :H2], preferred_element_type=F32)
            acc2_ref[:, H2:] += jnp.dot(vt_last, pb_last[:, H2:], preferred_element_type=F32)
        l_ref[...] = alpha * l_ref[...] + lsum
        m_ref[...] = m_new

    @pl.when(i > 0)
    def _():
        block(ratio * i - 2, ratio + 2, 0, plain=1, half_last=True, first=True)

    @pl.when(i == 0)
    def _():
        block(0, ratio, 1, half_last=True, first=True)

    n_far = jnp.maximum(ratio * i - 2, 0)

    def far_step(jj, carry):
        block(FAR_STEP * jj, FAR_STEP)
        return carry

    lax.fori_loop(0, n_far // FAR_STEP, far_step, 0)
    done = (n_far // FAR_STEP) * FAR_STEP
    size = FAR_STEP // 2
    while size >= 2:
        take = (n_far - done) >= size

        @pl.when(take)
        def _(done=done, size=size):
            block(done, size)

        done = done + jnp.where(take, size, 0)
        size //= 2

    lv = lamv_ref[...]
    lam = (jnp.exp(jnp.sum(lv[0:1] * lv[1:2], axis=1, keepdims=True))
           - jnp.exp(jnp.sum(lv[2:3] * lv[3:4], axis=1, keepdims=True)) + lam_init)
    l = l_ref[...]
    d = acc1_ref[...] / l[:, :T] - lam * (acc2_ref[...] / l[:, T:])
    ms = jnp.mean(d * d, axis=0, keepdims=True)
    y = d * lax.rsqrt(ms + EPS)
    o_ref[0] = (y.T * sg_ref[...]).astype(o_ref.dtype)


def _diff_attn(online, lam_init, q, k, vt, bias, lamv, sg):
    b, s, _ = q.shape
    T = Q_TILE
    TK = ATT_TILE
    nq = s // T
    return pl.pallas_call(
        functools.partial(_attn_body, lam_init, online),
        grid=(b, N_HEADS, nq),
        in_specs=[
            pl.BlockSpec(lamv.shape, lambda bi, h, i: (0, 0)),
            pl.BlockSpec((1, T, 2 * HEAD_DIM), lambda bi, h, i: (bi, i, h)),
            pl.BlockSpec((1, s, 2 * HEAD_DIM), lambda bi, h, i: (bi, 0, h)),
            pl.BlockSpec((1, 1, s // TK, 2 * HEAD_DIM, TK), lambda bi, h, i: (bi, h, 0, 0, 0)),
            pl.BlockSpec((bias.shape[0], 1, TK, T), lambda bi, h, i: (0, h, 0, 0),
                         pipeline_mode=pl.Buffered(1)),
            pl.BlockSpec(sg.shape, lambda bi, h, i: (0, 0)),
        ],
        out_specs=pl.BlockSpec((1, T, 2 * HEAD_DIM), lambda bi, h, i: (bi, i, h)),
        out_shape=jax.ShapeDtypeStruct((b, s, DIFF_WIDTH), BF16),
        scratch_shapes=[pltpu.VMEM((2 * HEAD_DIM, 2 * T), BF16), pltpu.VMEM((1, 2 * T), F32),
                        pltpu.VMEM((1, 2 * T), F32), pltpu.VMEM((2 * HEAD_DIM, T), F32),
                        pltpu.VMEM((2 * HEAD_DIM, T), F32)],
        compiler_params=pltpu.CompilerParams(dimension_semantics=("arbitrary", "arbitrary", "arbitrary"),
                                             vmem_limit_bytes=VMEM_LIMIT),
        name="diff_attn_online" if online else "diff_attn",
    )(lamv, q, k, vt, bias, sg)


def _rel_bucket_np(n):
    max_exact = NUM_BUCKETS // 2
    nf = np.maximum(n, 1).astype(np.float32)
    large = max_exact + (np.log(nf / max_exact) / math.log(MAX_DISTANCE / max_exact)
                         * (NUM_BUCKETS - max_exact)).astype(np.int32)
    large = np.minimum(large, NUM_BUCKETS - 1)
    return np.where(n < max_exact, n, large)


def _bias_body(kinds, tab_ref, idx_ref, o_ref):
    h = pl.program_id(0)
    for t, tile_kinds in enumerate(kinds):
        for i, row_kinds in enumerate(tile_kinds):
            for j, kind in enumerate(row_kinds):
                rows, cols = slice(i * LANES, (i + 1) * LANES), slice(j * LANES, (j + 1) * LANES)
                if kind != "band":
                    fill = NEG_BIG if kind == "masked" else 0.0
                    o_ref[t, 0, rows, cols] = jnp.full((LANES, LANES), fill, F32)
                    continue
                idx = idx_ref[t, rows, cols]
                acc = jnp.zeros(idx.shape, F32)
                for b in range(NUM_BUCKETS):
                    acc = jnp.where(idx == b, tab_ref[b, h], acc)
                o_ref[t, 0, rows, cols] = jnp.where(idx < 0, NEG_BIG, acc)


def _bias_tiles(tab):
    T, TK = Q_TILE, ATT_TILE
    n_tiles = T // TK + 1
    c = np.arange(TK)[:, None]
    r = np.arange(T)[None, :]
    dist = [r - c + TK * (1 - m) for m in range(n_tiles)]
    idx = jnp.asarray(np.stack([np.where(d >= 0, _rel_bucket_np(np.maximum(d, 0)), -1)
                                for d in dist]).astype(np.int32))

    def kind(d):
        return "masked" if (d < 0).all() else "zero" if (d >= MAX_DISTANCE).all() else "band"

    kinds = tuple(tuple(tuple(kind(d[i:i + LANES, j:j + LANES]) for j in range(0, T, LANES))
                        for i in range(0, TK, LANES)) for d in dist)
    return pl.pallas_call(
        functools.partial(_bias_body, kinds),
        grid=(N_HEADS,),
        in_specs=[pl.BlockSpec(memory_space=pltpu.SMEM),
                  pl.BlockSpec((n_tiles, TK, T), lambda h: (0, 0, 0))],
        out_specs=pl.BlockSpec((n_tiles, 1, TK, T), lambda h: (0, h, 0, 0)),
        out_shape=jax.ShapeDtypeStruct((n_tiles, N_HEADS, TK, T), F32),
        name="bias_tiles",
    )(tab, idx)


def kernel(x, rel_bias, ffn1_norm, ffn1_w_gate, ffn1_w_up, ffn1_w_down, mix_norm, w_in, conv_w, conv_b,
           conv_norm, q_norm, k_norm, lambda_q1, lambda_k1, lambda_q2, lambda_k2, subln_norm, w_out,
           ffn2_norm, ffn2_w_gate, ffn2_w_up, ffn2_w_down):
    B, S, _ = x.shape
    depth = w_in.shape[0]
    assert S % Q_TILE == 0 and Q_TILE == 2 * ATT_TILE and ROW_TILE == ATT_TILE and ATT_TILE >= MAX_DISTANCE

    grp = np.arange(GROUP_MM) // HEAD_DIM
    gsum = jnp.asarray(grp[:, None] == grp[None, :], BF16)
    tab = (rel_bias - rel_bias[NUM_BUCKETS - 1:]).astype(F32) * LOG2E
    bias = _bias_tiles(tab)
    row2 = lambda a: a.reshape(1, -1).astype(F32)

    xf = x.reshape(B * S, D_MODEL)
    for l in range(depth):
        lam_init = 0.8 - 0.6 * math.exp(-0.3 * l)
        qg = q_norm[l].astype(F32) * (LOG2E / math.sqrt(HEAD_DIM))
        kg = k_norm[l].astype(F32)
        qkg = jnp.concatenate([jnp.tile(qg, 2 * N_HEADS), jnp.tile(kg, 2 * N_HEADS)]).reshape(1, -1)
        score_bound = (HEAD_DIM * 1.02) * jnp.max(jnp.abs(qg)) * jnp.max(jnp.abs(kg)) + jnp.max(jnp.abs(tab))
        cw8 = jnp.broadcast_to(conv_w[l].astype(F32)[:, None, :], (CONV_K, SUBLANES, CONV_CH))
        x1, c, q, k, vt = _ffn_proj(xf, row2(ffn1_norm[l]), ffn1_w_gate[l].astype(BF16),
                                   ffn1_w_up[l].astype(BF16), ffn1_w_down[l].astype(BF16),
                                   row2(mix_norm[l]), w_in[l].astype(BF16), gsum, qkg, cw8,
                                   row2(conv_b[l]), row2(conv_norm[l]), S)
        lamv = jnp.stack([lambda_q1[l], lambda_k1[l], lambda_q2[l], lambda_k2[l]]).astype(F32)
        sg = row2(subln_norm[l]) * (1.0 - lam_init)
        att = lax.cond(score_bound <= SAFE_EXP2_BOUND,
                       functools.partial(_diff_attn, False, lam_init),
                       functools.partial(_diff_attn, True, lam_init),
                       q.reshape(B, S, DIFF_WIDTH), k.reshape(B, S, DIFF_WIDTH), vt, bias, lamv, sg)
        xf = _mix_ffn(x1, c, att.reshape(B * S, DIFF_WIDTH), w_out[l, :CONV_CH].astype(BF16),
                      w_out[l, CONV_CH:].astype(BF16), row2(ffn2_norm[l]), ffn2_w_gate[l].astype(BF16),
                      ffn2_w_up[l].astype(BF16), ffn2_w_down[l].astype(BF16))
    return xf.reshape(B, S, D_MODEL)
```

```python
import functools
import math

import jax
import jax.numpy as jnp
import numpy as np
from jax import lax
from jax.experimental import pallas as pl
from jax.experimental.pallas import tpu as pltpu

F32 = jnp.float32
BF16 = jnp.bfloat16

D_MODEL = 1024
CONV_CH = 512
CONV_K = 31
DIFF_WIDTH = 512
HEAD_DIM = 64
N_HEADS = 4
D_FF = 2816
NUM_BUCKETS = 32
MAX_DISTANCE = 128
EPS = 1e-6
LOG2E = 1.4426950408889634
NEG_BIG = -1e30
SAFE_EXP2_BOUND = 90.0

FF_CHUNK = 256
N_FF_CHUNKS = D_FF // FF_CHUNK
ROW_TILE = 512
MIX_TILE = 1024
ATT_TILE = 512
Q_TILE = 1024
FAR_STEP = 4
SIDE_WORK_LAG = 2
LANES = 128
SUBLANES = 8
CONV_HALO = 32
CONV_ROWS = 16
GROUP_MM = 256
VMEM_LIMIT = 56 * 1024 * 1024


def _rms(x, g):
    ms = jnp.mean(x * x, axis=-1, keepdims=True)
    return x * lax.rsqrt(ms + EPS) * g


def _zero_after(v):
    bits = pltpu.bitcast(v[0:1, :], jnp.uint32)
    return pltpu.bitcast((bits >> 16) >> 16, F32)


def _ffn(x, g_ref, wg_ref, wu_ref, wd_ref, act_ref, side_work=None):
    h = _rms(x, g_ref[...]).astype(BF16)
    zs = [None] * SIDE_WORK_LAG
    for c in range(N_FF_CHUNKS):
        lo = c * FF_CHUNK
        gate = jnp.dot(h, wg_ref[:, lo:lo + FF_CHUNK], preferred_element_type=F32)
        up = jnp.dot(h, wu_ref[:, lo:lo + FF_CHUNK], preferred_element_type=F32)
        z = zs.pop(0)
        if z is not None:
            up = up + z
        act_ref[:, c * FF_CHUNK:(c + 1) * FF_CHUNK] = (gate * jax.nn.sigmoid(gate) * up).astype(BF16)
        zs.append(side_work(c, gate) if side_work is not None else None)
    y = jnp.dot(act_ref[...], wd_ref[...], preferred_element_type=F32)
    return x + 0.5 * y


def _conv_prev_tile(ubuf, ush, cw_ref, cb_ref, cn_ref, c_ref):
    tm = c_ref.shape[0]
    base = CONV_HALO - (CONV_K - 1)
    n_pieces = N_FF_CHUNKS
    rows_per = -(-tm // (n_pieces * SUBLANES)) * SUBLANES

    def rows_after(r0, nr, z):
        groups = nr // SUBLANES
        bias = cb_ref[...] + z
        acc = jnp.broadcast_to(bias, (groups, SUBLANES, CONV_CH))
        for j in range(CONV_K):
            a, b = divmod(base + j, SUBLANES)
            lo = r0 + a * SUBLANES
            rows = ubuf[lo:lo + nr, :] if b == 0 else ush[b - 1, lo:lo + nr, :]
            acc = acc + cw_ref[j][None] * rows.reshape(groups, SUBLANES, CONV_CH)
        c = _rms(acc.reshape(nr, CONV_CH), cn_ref[...])
        c = c * jax.nn.sigmoid(c)
        c_ref[r0:r0 + nr, :] = c.astype(c_ref.dtype)
        return _zero_after(jnp.sum(c, axis=0, keepdims=True))

    def piece(p, gate):
        zg = _zero_after(gate)
        z = jnp.concatenate([zg] * (CONV_CH // FF_CHUNK), axis=1)
        r0 = p * rows_per
        end = min(r0 + rows_per, tm)
        while r0 < end:
            nr = min(CONV_ROWS, end - r0)
            z = rows_after(r0, nr, z)
            r0 += nr
        return z[:, :FF_CHUNK]

    return piece


def _ffn_proj_body(seq_tiles, x_ref, n1_ref, wg_ref, wu_ref, wd_ref, nm_ref, win_ref, gsum_ref, qkg_ref,
                   cw_ref, cb_ref, cn_ref, xo_ref, c_ref, q_ref, k_ref, vt_ref, act_ref, ubuf, ush):
    i = pl.program_id(0)
    tm = x_ref.shape[0]

    @pl.when(i == 0)
    def _():
        ubuf[...] = jnp.zeros(ubuf.shape, F32)
        ush[...] = jnp.zeros(ush.shape, F32)

    conv_piece = _conv_prev_tile(ubuf, ush, cw_ref, cb_ref, cn_ref, c_ref)
    x1 = _ffn(x_ref[...], n1_ref, wg_ref, wu_ref, wd_ref, act_ref, side_work=conv_piece)
    xo_ref[...] = x1
    h = _rms(x1, nm_ref[...]).astype(BF16)
    ag = jnp.dot(h, win_ref[:, 0:2 * CONV_CH], preferred_element_type=F32)
    u = ag[:, :CONV_CH] * jax.nn.sigmoid(ag[:, CONV_CH:])
    qk =jnp.dot(h, win_ref[:, 2 * CONV_CH:2 * CONV_CH + 2 * DIFF_WIDTH], preferred_element_type=F32)
    sq = (qk * qk).astype(BF16)
    ss = jnp.concatenate(
        [jnp.dot(sq[:, c * GROUP_MM:(c + 1) * GROUP_MM], gsum_ref[...], preferred_element_type=F32)
         for c in range(2 * DIFF_WIDTH // GROUP_MM)], axis=1)
    qkn = qk * lax.rsqrt(ss * (1.0 / HEAD_DIM) + EPS) * qkg_ref[...]
    q_ref[...] = qkn[:, :DIFF_WIDTH].astype(BF16)
    k_ref[...] = qkn[:, DIFF_WIDTH:].astype(BF16)
    v = jnp.dot(h, win_ref[:, 2 * CONV_CH + 2 * DIFF_WIDTH:], preferred_element_type=F32)
    vt_ref[0, :, 0] = v.T.reshape(N_HEADS, 2 * HEAD_DIM, tm).astype(BF16)

    tail = ubuf[tm:tm + CONV_HALO, :]
    ubuf[0:CONV_HALO, :] = jnp.where(i % seq_tiles == 0, 0.0, tail)
    ubuf[CONV_HALO:, :] = u
    for b in range(1, SUBLANES):
        ush[b - 1] = ubuf[b:b + ush.shape[1], :]


def _const_spec(shape):
    nd = len(shape)
    return pl.BlockSpec(shape, lambda i: (0,) * nd, pipeline_mode=pl.Buffered(1))


def _ffn_proj(x, n1, wg, wu, wd, nm, win, gsum, qkg, cw, cb, cn, seq):
    t = x.shape[0]
    tm = ROW_TILE
    n = t // tm
    cur = lambda w: pl.BlockSpec((tm, w), lambda i: (jnp.minimum(i, n - 1), 0))
    prev = lambda w: pl.BlockSpec((tm, w), lambda i: (jnp.maximum(i - 1, 0), 0))
    consts = (n1, wg, wu, wd, nm, win, gsum, qkg, cw, cb, cn)
    seq_tiles = seq // tm

    def vt_index(i):
        ii = jnp.minimum(i, n - 1)
        return (ii // seq_tiles, 0, ii % seq_tiles, 0, 0)

    return pl.pallas_call(
        functools.partial(_ffn_proj_body, seq_tiles),
        grid=(n + 1,),
        in_specs=[cur(D_MODEL)] + [_const_spec(a.shape) for a in consts],
        out_specs=[cur(D_MODEL), prev(CONV_CH), cur(DIFF_WIDTH), cur(DIFF_WIDTH),
                   pl.BlockSpec((1, N_HEADS, 1, 2 * HEAD_DIM, tm), vt_index)],
        out_shape=[jax.ShapeDtypeStruct((t, D_MODEL), F32), jax.ShapeDtypeStruct((t, CONV_CH), BF16),
                   jax.ShapeDtypeStruct((t, DIFF_WIDTH), BF16), jax.ShapeDtypeStruct((t, DIFF_WIDTH), BF16),
                   jax.ShapeDtypeStruct((t // seq, N_HEADS, seq_tiles, 2 * HEAD_DIM, tm), BF16)],
        scratch_shapes=[pltpu.VMEM((tm, D_FF), BF16), pltpu.VMEM((tm + CONV_HALO, CONV_CH), F32),
                        pltpu.VMEM((SUBLANES - 1, tm + CONV_HALO - SUBLANES, CONV_CH), F32)],
        compiler_params=pltpu.CompilerParams(dimension_semantics=("arbitrary",),
                                             vmem_limit_bytes=VMEM_LIMIT),
        name="ffn_proj",
    )(x, *consts)


def _mix_ffn_body(x_ref, c_ref, att_ref, woc_ref, wod_ref, n2_ref, wg_ref, wu_ref, wd_ref, o_ref,
                  act_ref):
    mix = (jnp.dot(c_ref[...], woc_ref[...], preferred_element_type=F32)
           + jnp.dot(att_ref[...], wod_ref[...], preferred_element_type=F32))
    x1 = x_ref[...] + mix
    o_ref[...] = _ffn(x1, n2_ref, wg_ref, wu_ref, wd_ref, act_ref)


def _mix_ffn(x, c, att, woc, wod, n2, wg, wu, wd):
    t = x.shape[0]
    tm = MIX_TILE
    row = lambda w: pl.BlockSpec((tm, w), lambda i: (i, 0))
    consts = (woc, wod, n2, wg, wu, wd)
    return pl.pallas_call(
        _mix_ffn_body,
        grid=(t // tm,),
        in_specs=[row(D_MODEL), row(CONV_CH), row(DIFF_WIDTH)] + [_const_spec(a.shape) for a in consts],
        out_specs=row(D_MODEL),
        out_shape=jax.ShapeDtypeStruct((t, D_MODEL), F32),
        scratch_shapes=[pltpu.VMEM((tm, D_FF), BF16)],
        compiler_params=pltpu.CompilerParams(dimension_semantics=("arbitrary",),
                                             vmem_limit_bytes=VMEM_LIMIT),
        name="mix_ffn",
    )(x, c, att, *consts)


def _attn_body(lam_init, online, lamv_ref, q_ref, k_ref, vt_ref, bias_ref, sg_ref, o_ref,
               qbd_ref, m_ref, l_ref, acc1_ref, acc2_ref):
    T = Q_TILE
    TK = ATT_TILE
    ratio = T // TK
    i = pl.program_id(2)

    qt = q_ref[0].astype(F32).T
    row = lax.broadcasted_iota(jnp.int32, qt.shape, 0)
    qbd_ref[:, :T] = jnp.where(row < HEAD_DIM, qt, 0.0).astype(BF16)
    qbd_ref[:, T:] = jnp.where(row >= HEAD_DIM, qt, 0.0).astype(BF16)
    if online:
        m_ref[...] = jnp.full(m_ref.shape, NEG_BIG, F32)
        l_ref[...] = jnp.zeros(l_ref.shape, F32)
        acc1_ref[...] = jnp.zeros(acc1_ref.shape, F32)
        acc2_ref[...] = jnp.zeros(acc2_ref.shape, F32)

    H2 = T // 2

    def on_upper_half(v, fill):
        pad = jnp.full((1, H2), fill, F32)
        return jnp.concatenate([pad, v[:, :H2], pad, v[:, H2:]], axis=1)

    def block(j, nblk, first_bias=None, plain=0, half_last=False, first=False):
        n_main = nblk - 1 if half_last else nblk
        kb = k_ref[0, pl.ds(pl.multiple_of(j * TK, TK), nblk * TK), :]
        s = jnp.dot(kb[:n_main * TK], qbd_ref[...], preferred_element_type=F32)
        if first_bias is not None:
            nb = n_main - plain
            b = bias_ref[first_bias:first_bias + nb, 0].reshape(nb * TK, T)
            biased = s[plain * TK:] + jnp.concatenate([b, b], axis=1)
            s = biased if plain == 0 else jnp.concatenate([s[:plain * TK], biased], axis=0)
        vtb = jnp.concatenate([vt_ref[0, 0, j + t] for t in range(n_main)], axis=1)
        if half_last:
            q_half = jnp.concatenate([qbd_ref[:, H2:T], qbd_ref[:, T + H2:]], axis=1)
            b_last = bias_ref[first_bias + n_main - plain, 0][:, H2:]
            s_last = (jnp.dot(kb[n_main * TK:], q_half, preferred_element_type=F32)
                      + jnp.concatenate([b_last, b_last], axis=1))
            vt_last = vt_ref[0, 0, j + n_main]
        if not online:
            p = jnp.exp2(s)
            lsum = jnp.sum(p, axis=0, keepdims=True)
            pb = p.astype(BF16)
            pv1 = jnp.dot(vtb, pb[:, :T], preferred_element_type=F32)
            pv2 = jnp.dot(vtb, pb[:, T:], preferred_element_type=F32)
            if half_last:
                p_last = jnp.exp2(s_last)
                lsum = lsum + on_upper_half(jnp.sum(p_last, axis=0, keepdims=True), 0.0)
                pb_last = p_last.astype(BF16)
            acc1_ref[...] = pv1 if first else acc1_ref[...] + pv1
            acc2_ref[...] = pv2 if first else acc2_ref[...] + pv2
            l_ref[...] = lsum if first else l_ref[...] + lsum
            if half_last:
                acc1_ref[:, H2:] += jnp.dot(vt_last, pb_last[:, :H2], preferred_element_type=F32)
                acc2_ref[:, H2:] += jnp.dot(vt_last, pb_last[:, H2:], preferred_element_type=F32)
            return
        m_old = m_ref[...]
        cmax = jnp.max(s, axis=0, keepdims=True)
        if half_last:
            cmax = jnp.maximum(cmax, on_upper_half(jnp.max(s_last, axis=0, keepdims=True), NEG_BIG))
        m_new = jnp.maximum(m_old, cmax)
        alpha = jnp.exp2(m_old - m_new)
        p = jnp.exp2(s - m_new)
        lsum = jnp.sum(p, axis=0, keepdims=True)
        pb = p.astype(BF16)
        acc1_ref[...] = alpha[:, :T] * acc1_ref[...] + jnp.dot(vtb, pb[:, :T], preferred_element_type=F32)
        acc2_ref[...] = alpha[:, T:] * acc2_ref[...] + jnp.dot(vtb, pb[:, T:], preferred_element_type=F32)
        if half_last:
            m_half = jnp.concatenate([m_new[:, H2:T], m_new[:, T + H2:]], axis=1)
            p_last = jnp.exp2(s_last - m_half)
            lsum = lsum + on_upper_half(jnp.sum(p_last, axis=0, keepdims=True), 0.0)
            pb_last = p_last.astype(BF16)
            acc1_ref[:, H2:] += jnp.dot(vt_last, pb_last[:, :H2], preferred_element_type=F32)
            acc2_ref[:, H2:] += jnp.dot(vt_last, pb_last[:, H2:], preferred_element_type=F32)
        l_ref[...] = alpha * l_ref[...] + lsum
        m_ref[...] = m_new

    @pl.when(i > 0)
    def _():
        block(ratio * i - 2, ratio + 2, 0, plain=1, half_last=True, first=True)

    @pl.when(i == 0)
    def _():
        block(0, ratio, 1, half_last=True, first=True)

    n_far = jnp.maximum(ratio * i - 2, 0)

    def far_step(jj, carry):
        block(FAR_STEP * jj, FAR_STEP)
        return carry

    lax.fori_loop(0, n_far // FAR_STEP, far_step, 0)
    done = (n_far // FAR_STEP) * FAR_STEP
    size = FAR_STEP // 2
    while size >= 2:
        take = (n_far - done) >= size

        @pl.when(take)
        def _(done=done, size=size):
            block(done, size)

        done = done + jnp.where(take, size, 0)
        size //= 2

    lv = lamv_ref[...]
    lam = (jnp.exp(jnp.sum(lv[0:1] * lv[1:2], axis=1, keepdims=True))
           - jnp.exp(jnp.sum(lv[2:3] * lv[3:4], axis=1, keepdims=True)) + lam_init)
    l = l_ref[...]
    d = acc1_ref[...] / l[:, :T] - lam * (acc2_ref[...] / l[:, T:])
    ms = jnp.mean(d * d, axis=0, keepdims=True)
    y = d * lax.rsqrt(ms + EPS)
    o_ref[0] = (y.T * sg_ref[...]).astype(o_ref.dtype)


def _diff_attn(online, lam_init, q, k, vt, bias, lamv, sg):
    b, s, _ = q.shape
    T = Q_TILE
    TK = ATT_TILE
    nq = s // T
    return pl.pallas_call(
        functools.partial(_attn_body, lam_init, online),
        grid=(b, N_HEADS, nq),
        in_specs=[
            pl.BlockSpec(lamv.shape, lambda bi, h, i: (0, 0)),
            pl.BlockSpec((1, T, 2 * HEAD_DIM), lambda bi, h, i: (bi, i, h)),
            pl.BlockSpec((1, s, 2 * HEAD_DIM), lambda bi, h, i: (bi, 0, h)),
            pl.BlockSpec((1, 1, s // TK, 2 * HEAD_DIM, TK), lambda bi, h, i: (bi, h, 0, 0, 0)),
            pl.BlockSpec((bias.shape[0], 1, TK, T), lambda bi, h, i: (0, h, 0, 0),
                         pipeline_mode=pl.Buffered(1)),
            pl.BlockSpec(sg.shape, lambda bi, h, i: (0, 0)),
        ],
        out_specs=pl.BlockSpec((1, T, 2 * HEAD_DIM), lambda bi, h, i: (bi, i, h)),
        out_shape=jax.ShapeDtypeStruct((b, s, DIFF_WIDTH), BF16),
        scratch_shapes=[pltpu.VMEM((2 * HEAD_DIM, 2 * T), BF16), pltpu.VMEM((1, 2 * T), F32),
                        pltpu.VMEM((1, 2 * T), F32), pltpu.VMEM((2 * HEAD_DIM, T), F32),
                        pltpu.VMEM((2 * HEAD_DIM, T), F32)],
        compiler_params=pltpu.CompilerParams(dimension_semantics=("arbitrary", "arbitrary", "arbitrary"),
                                             vmem_limit_bytes=VMEM_LIMIT),
        name="diff_attn_online" if online else "diff_attn",
    )(lamv, q, k, vt, bias, sg)


def _rel_bucket_np(n):
    max_exact = NUM_BUCKETS // 2
    nf = np.maximum(n, 1).astype(np.float32)
    large = max_exact + (np.log(nf / max_exact) / math.log(MAX_DISTANCE / max_exact)
                         * (NUM_BUCKETS - max_exact)).astype(np.int32)
    large = np.minimum(large, NUM_BUCKETS - 1)
    return np.where(n < max_exact, n, large)


def _bias_body(kinds, tab_ref, idx_ref, o_ref):
    h = pl.program_id(0)
    for t, tile_kinds in enumerate(kinds):
        for i, row_kinds in enumerate(tile_kinds):
            for j, kind in enumerate(row_kinds):
                rows, cols = slice(i * LANES, (i + 1) * LANES), slice(j * LANES, (j + 1) * LANES)
                if kind != "band":
                    fill = NEG_BIG if kind == "masked" else 0.0
                    o_ref[t, 0, rows, cols] = jnp.full((LANES, LANES), fill, F32)
                    continue
                idx = idx_ref[t, rows, cols]
                acc = jnp.zeros(idx.shape, F32)
                for b in range(NUM_BUCKETS):
                    acc = jnp.where(idx == b, tab_ref[b, h], acc)
                o_ref[t, 0, rows, cols] = jnp.where(idx < 0, NEG_BIG, acc)


def _bias_tiles(tab):
    T, TK = Q_TILE, ATT_TILE
    n_tiles = T // TK + 1
    c = np.arange(TK)[:, None]
    r = np.arange(T)[None, :]
    dist = [r - c + TK * (1 - m) for m in range(n_tiles)]
    idx = jnp.asarray(np.stack([np.where(d >= 0, _rel_bucket_np(np.maximum(d, 0)), -1)
                                for d in dist]).astype(np.int32))

    def kind(d):
        return "masked" if (d < 0).all() else "zero" if (d >= MAX_DISTANCE).all() else "band"

    kinds = tuple(tuple(tuple(kind(d[i:i + LANES, j:j + LANES]) for j in range(0, T, LANES))
                        for i in range(0, TK, LANES)) for d in dist)
    return pl.pallas_call(
        functools.partial(_bias_body, kinds),
        grid=(N_HEADS,),
        in_specs=[pl.BlockSpec(memory_space=pltpu.SMEM),
                  pl.BlockSpec((n_tiles, TK, T), lambda h: (0, 0, 0))],
        out_specs=pl.BlockSpec((n_tiles, 1, TK, T), lambda h: (0, h, 0, 0)),
        out_shape=jax.ShapeDtypeStruct((n_tiles, N_HEADS, TK, T), F32),
        name="bias_tiles",
    )(tab, idx)


def kernel(x, rel_bias, ffn1_norm, ffn1_w_gate, ffn1_w_up, ffn1_w_down, mix_norm, w_in, conv_w, conv_b,
           conv_norm, q_norm, k_norm, lambda_q1, lambda_k1, lambda_q2, lambda_k2, subln_norm, w_out,
           ffn2_norm, ffn2_w_gate, ffn2_w_up, ffn2_w_down):
    B, S, _ = x.shape
    depth = w_in.shape[0]
    assert S % Q_TILE == 0 and Q_TILE == 2 * ATT_TILE and ROW_TILE == ATT_TILE and ATT_TILE >= MAX_DISTANCE

    grp = np.arange(GROUP_MM) // HEAD_DIM
    gsum = jnp.asarray(grp[:, None] == grp[None, :], BF16)
    tab = (rel_bias - rel_bias[NUM_BUCKETS - 1:]).astype(F32) * LOG2E
    bias = _bias_tiles(tab)
    row2 = lambda a: a.reshape(1, -1).astype(F32)

    xf = x.reshape(B * S, D_MODEL)
    for l in range(depth):
        lam_init = 0.8 - 0.6 * math.exp(-0.3 * l)
        qg = q_norm[l].astype(F32) * (LOG2E / math.sqrt(HEAD_DIM))
        kg = k_norm[l].astype(F32)
        qkg = jnp.concatenate([jnp.tile(qg, 2 * N_HEADS), jnp.tile(kg, 2 * N_HEADS)]).reshape(1, -1)
        score_bound = (HEAD_DIM * 1.02) * jnp.max(jnp.abs(qg)) * jnp.max(jnp.abs(kg)) + jnp.max(jnp.abs(tab))
        cw8 = jnp.broadcast_to(conv_w[l].astype(F32)[:, None, :], (CONV_K, SUBLANES, CONV_CH))
        x1, c, q, k, vt = _ffn_proj(xf, row2(ffn1_norm[l]), ffn1_w_gate[l].astype(BF16),
                                   ffn1_w_up[l].astype(BF16), ffn1_w_down[l].astype(BF16),
                                   row2(mix_norm[l]), w_in[l].astype(BF16), gsum, qkg, cw8,
                                   row2(conv_b[l]), row2(conv_norm[l]), S)
        lamv = jnp.stack([lambda_q1[l], lambda_k1[l], lambda_q2[l], lambda_k2[l]]).astype(F32)
        sg = row2(subln_norm[l]) * (1.0 - lam_init)
        att = lax.cond(score_bound <= SAFE_EXP2_BOUND,
                       functools.partial(_diff_attn, False, lam_init),
                       functools.partial(_diff_attn, True, lam_init),
                       q.reshape(B, S, DIFF_WIDTH), k.reshape(B, S, DIFF_WIDTH), vt, bias, lamv, sg)
        xf = _mix_ffn(x1, c, att.reshape(B * S, DIFF_WIDTH), w_out[l, :CONV_CH].astype(BF16),
                      w_out[l, CONV_CH:].astype(BF16), row2(ffn2_norm[l]), ffn2_w_gate[l].astype(BF16),
                      ffn2_w_up[l].astype(BF16), ffn2_w_down[l].astype(BF16))
    return xf.reshape(B, S, D_MODEL)
```

```python
import functools
import math

import jax
import jax.numpy as jnp
import numpy as np
from jax import lax
from jax.experimental import pallas as pl
from jax.experimental.pallas import tpu as pltpu

F32 = jnp.float32
BF16 = jnp.bfloat16

D_MODEL = 1024
CONV_CH = 512
CONV_K = 31
DIFF_WIDTH = 512
HEAD_DIM = 64
N_HEADS = 4
D_FF = 2816
NUM_BUCKETS = 32
MAX_DISTANCE = 128
EPS = 1e-6
LOG2E = 1.4426950408889634
NEG_BIG = -1e30
SAFE_EXP2_BOUND = 90.0

FF_CHUNK = 256
N_FF_CHUNKS = D_FF // FF_CHUNK
ROW_TILE = 512
MIX_TILE = 1024
ATT_TILE = 512
Q_TILE = 1024
FAR_STEP = 4
SIDE_WORK_LAG = 2
LANES = 128
SUBLANES = 8
CONV_HALO = 32
CONV_ROWS = 16
GROUP_MM = 256
VMEM_LIMIT = 56 * 1024 * 1024


def _rms(x, g):
    ms = jnp.mean(x * x, axis=-1, keepdims=True)
    return x * lax.rsqrt(ms + EPS) * g


def _zero_after(v):
    bits = pltpu.bitcast(v[0:1, :], jnp.uint32)
    return pltpu.bitcast((bits >> 16) >> 16, F32)


def _ffn(x, g_ref, wg_ref, wu_ref, wd_ref, act_ref, side_work=None):
    h = _rms(x, g_ref[...]).astype(BF16)
    zs = [None] * SIDE_WORK_LAG
    for c in range(N_FF_CHUNKS):
        lo = c * FF_CHUNK
        gate = jnp.dot(h, wg_ref[:, lo:lo + FF_CHUNK], preferred_element_type=F32)
        up = jnp.dot(h, wu_ref[:, lo:lo + FF_CHUNK], preferred_element_type=F32)
        z = zs.pop(0)
        if z is not None:
            up = up + z
        act_ref[:, lo:lo + FF_CHUNK] = (gate * jax.nn.sigmoid(gate) * up).astype(BF16)
        zs.append(side_work(c, gate) if side_work is not None else None)
    y = jnp.dot(act_ref[...], wd_ref[...], preferred_element_type=F32)
    return x + 0.5 * y


def _conv_prev_tile(ubuf, ush, cw_ref, cb_ref, cn_ref, c_ref):
    tm = c_ref.shape[0]
    base = CONV_HALO - (CONV_K - 1)
    n_pieces = N_FF_CHUNKS
    rows_per = -(-tm // (n_pieces * SUBLANES)) * SUBLANES

    def rows_after(r0, nr, z):
        groups = nr // SUBLANES
        bias = cb_ref[...] + z
        acc = jnp.broadcast_to(bias, (groups, SUBLANES, CONV_CH))
        for j in range(CONV_K):
            a, b = divmod(base + j, SUBLANES)
            lo = r0 + a * SUBLANES
            rows = ubuf[lo:lo + nr, :] if b == 0 else ush[b - 1, lo:lo + nr, :]
            acc = acc + cw_ref[j][None] * rows.reshape(groups, SUBLANES, CONV_CH)
        c = _rms(acc.reshape(nr, CONV_CH), cn_ref[...])
        c = c * jax.nn.sigmoid(c)
        c_ref[r0:r0 + nr, :] = c.astype(c_ref.dtype)
        return _zero_after(jnp.sum(c, axis=0, keepdims=True))

    def piece(p, gate):
        zg = _zero_after(gate)
        z = jnp.concatenate([zg] * (CONV_CH // FF_CHUNK), axis=1)
        r0 = p * rows_per
        end = min(r0 + rows_per, tm)
        while r0 < end:
            nr = min(CONV_ROWS, end - r0)
            z = rows_after(r0, nr, z)
            r0 += nr
        return z[:, :FF_CHUNK]

    return piece


def _ffn_proj_body(seq_tiles, x_ref, n1_ref, wg_ref, wu_ref, wd_ref, nm_ref, win_ref, gsum_ref, qkg_ref,
                   cw_ref, cb_ref, cn_ref, xo_ref, c_ref, q_ref, k_ref, vt_ref, act_ref, ubuf, ush):
    i = pl.program_id(0)
    tm = x_ref.shape[0]

    @pl.when(i == 0)
    def _():
        ubuf[...] = jnp.zeros(ubuf.shape, F32)
        ush[...] = jnp.zeros(ush.shape, F32)

    conv_piece = _conv_prev_tile(ubuf, ush, cw_ref, cb_ref, cn_ref, c_ref)
    x1 = _ffn(x_ref[...], n1_ref, wg_ref, wu_ref, wd_ref, act_ref, side_work=conv_piece)
    xo_ref[...] = x1
    h = _rms(x1, nm_ref[...]).astype(BF16)
    ag = jnp.dot(h, win_ref[:, 0:2 * CONV_CH], preferred_element_type=F32)
    u = ag[:, :CONV_CH] * jax.nn.sigmoid(ag[:, CONV_CH:])
    qk = jnp.dot(h, win_ref[:, 2 * CONV_CH:2 * CONV_CH + 2 * DIFF_WIDTH], preferred_element_type=F32)
    sq = (qk * qk).astype(BF16)
    ss = jnp.concatenate(
        [jnp.dot(sq[:, c * GROUP_MM:(c + 1) * GROUP_MM], gsum_ref[...], preferred_element_type=F32)
         for c in range(2 * DIFF_WIDTH // GROUP_MM)], axis=1)
    qkn = qk * lax.rsqrt(ss * (1.0 / HEAD_DIM) + EPS) * qkg_ref[...]
    q_ref[...] = qkn[:, :DIFF_WIDTH].astype(BF16)
    k_ref[...] = qkn[:, DIFF_WIDTH:].astype(BF16)
    v = jnp.dot(h, win_ref[:, 2 * CONV_CH + 2 * DIFF_WIDTH:], preferred_element_type=F32)
    vt_ref[0, :, 0] = v.T.reshape(N_HEADS, 2 * HEAD_DIM, tm).astype(BF16)

    tail = ubuf[tm:tm + CONV_HALO, :]
    ubuf[0:CONV_HALO, :] = jnp.where(i % seq_tiles == 0, 0.0, tail)
    ubuf[CONV_HALO:, :] = u
    for b in range(1, SUBLANES):
        ush[b - 1] = ubuf[b:b + ush.shape[1], :]


def _const_spec(shape):
    nd = len(shape)
    return pl.BlockSpec(shape, lambda i: (0,) * nd, pipeline_mode=pl.Buffered(1))


def _ffn_proj(x, n1, wg, wu, wd, nm, win, gsum, qkg, cw, cb, cn, seq):
    t = x.shape[0]
    tm = ROW_TILE
    n = t // tm
    cur = lambda w: pl.BlockSpec((tm, w), lambda i: (jnp.minimum(i, n - 1), 0))
    prev = lambda w: pl.BlockSpec((tm, w), lambda i: (jnp.maximum(i - 1, 0), 0))
    consts = (n1, wg, wu, wd, nm, win, gsum, qkg, cw, cb, cn)
    seq_tiles = seq // tm

    def vt_index(i):
        ii = jnp.minimum(i, n - 1)
        return (ii // seq_tiles, 0, ii % seq_tiles, 0, 0)

    return pl.pallas_call(
        functools.partial(_ffn_proj_body, seq_tiles),
        grid=(n + 1,),
        in_specs=[cur(D_MODEL)] + [_const_spec(a.shape) for a in consts],
        out_specs=[cur(D_MODEL), prev(CONV_CH), cur(DIFF_WIDTH), cur(DIFF_WIDTH),
                   pl.BlockSpec((1, N_HEADS, 1, 2 * HEAD_DIM, tm), vt_index)],
        out_shape=[jax.ShapeDtypeStruct((t, D_MODEL), F32), jax.ShapeDtypeStruct((t, CONV_CH), BF16),
                   jax.ShapeDtypeStruct((t, DIFF_WIDTH), BF16), jax.ShapeDtypeStruct((t, DIFF_WIDTH), BF16),
                   jax.ShapeDtypeStruct((t // seq, N_HEADS, seq_tiles, 2 * HEAD_DIM, tm), BF16)],
        scratch_shapes=[pltpu.VMEM((tm, D_FF), BF16), pltpu.VMEM((tm + CONV_HALO, CONV_CH), F32),
                        pltpu.VMEM((SUBLANES - 1, tm + CONV_HALO - SUBLANES, CONV_CH), F32)],
        compiler_params=pltpu.CompilerParams(dimension_semantics=("arbitrary",),
                                             vmem_limit_bytes=VMEM_LIMIT),
        name="ffn_proj",
    )(x, *consts)


def _mix_ffn_body(x_ref, c_ref, att_ref, woc_ref, wod_ref, n2_ref, wg_ref, wu_ref, wd_ref, o_ref,
                  act_ref):
    mix = (jnp.dot(c_ref[...], woc_ref[...], preferred_element_type=F32)
           + jnp.dot(att_ref[...], wod_ref[...], preferred_element_type=F32))
    x1 = x_ref[...] + mix
    o_ref[...] = _ffn(x1, n2_ref, wg_ref, wu_ref, wd_ref, act_ref)


def _mix_ffn(x, c, att, woc, wod, n2, wg, wu, wd):
    t = x.shape[0]
    tm = MIX_TILE
    row = lambda w: pl.BlockSpec((tm, w), lambda i: (i, 0))
    consts = (woc, wod, n2, wg, wu, wd)
    return pl.pallas_call(
        _mix_ffn_body,
        grid=(t // tm,),
        in_specs=[row(D_MODEL), row(CONV_CH), row(DIFF_WIDTH)] + [_const_spec(a.shape) for a in consts],
        out_specs=row(D_MODEL),
        out_shape=jax.ShapeDtypeStruct((t, D_MODEL), F32),
        scratch_shapes=[pltpu.VMEM((tm, D_FF), BF16)],
        compiler_params=pltpu.CompilerParams(dimension_semantics=("arbitrary",),
                                             vmem_limit_bytes=VMEM_LIMIT),
        name="mix_ffn",
    )(x, c, att, *consts)


def _attn_body(lam_init, online, lamv_ref, q_ref, k_ref, vt_ref, bias_ref, sg_ref, o_ref,
               qbd_ref, m_ref, l_ref, acc1_ref, acc2_ref):
    T = Q_TILE
    TK = ATT_TILE
    ratio = T // TK
    i = pl.program_id(2)

    qt = q_ref[0].astype(F32).T
    row = lax.broadcasted_iota(jnp.int32, qt.shape, 0)
    qbd_ref[:, :T] = jnp.where(row < HEAD_DIM, qt, 0.0).astype(BF16)
    qbd_ref[:, T:] = jnp.where(row >= HEAD_DIM, qt, 0.0).astype(BF16)
    if online:
        m_ref[...] = jnp.full(m_ref.shape, NEG_BIG, F32)
        l_ref[...] = jnp.zeros(l_ref.shape, F32)
        acc1_ref[...] = jnp.zeros(acc1_ref.shape, F32)
        acc2_ref[...] = jnp.zeros(acc2_ref.shape, F32)

    H2 = T // 2

    def on_upper_half(v, fill):
        pad = jnp.full((1, H2), fill, F32)
        return jnp.concatenate([pad, v[:, :H2], pad, v[:, H2:]], axis=1)

    def block(j, nblk, first_bias=None, plain=0, half_last=False, first=False):
        n_main = nblk - 1 if half_last else nblk
        kb = k_ref[0, pl.ds(pl.multiple_of(j * TK, TK), nblk * TK), :]
        s = jnp.dot(kb[:n_main * TK], qbd_ref[...], preferred_element_type=F32)
        if first_bias is not None:
            nb = n_main - plain
            b = bias_ref[first_bias:first_bias + nb, 0].reshape(nb * TK, T)
            biased = s[plain * TK:] + jnp.concatenate([b, b], axis=1)
            s = biased if plain == 0 else jnp.concatenate([s[:plain * TK], biased], axis=0)
        vtb = jnp.concatenate([vt_ref[0, 0, j + t] for t in range(n_main)], axis=1)
        if half_last:
            q_half = jnp.concatenate([qbd_ref[:, H2:T], qbd_ref[:, T + H2:]], axis=1)
            b_last = bias_ref[first_bias + n_main - plain, 0][:, H2:]
            s_last = (jnp.dot(kb[n_main * TK:], q_half, preferred_element_type=F32)
                      + jnp.concatenate([b_last, b_last], axis=1))
            vt_last = vt_ref[0, 0, j + n_main]
        if not online:
            p = jnp.exp2(s)
            lsum = jnp.sum(p, axis=0, keepdims=True)
            pb = p.astype(BF16)
            pv1 = jnp.dot(vtb, pb[:, :T], preferred_element_type=F32)
            pv2 = jnp.dot(vtb, pb[:, T:], preferred_element_type=F32)
            if half_last:
                p_last = jnp.exp2(s_last)
                lsum = lsum + on_upper_half(jnp.sum(p_last, axis=0, keepdims=True), 0.0)
                pb_last = p_last.astype(BF16)
            acc1_ref[...] = pv1 if first else acc1_ref[...] + pv1
            acc2_ref[...] = pv2 if first else acc2_ref[...] + pv2
            l_ref[...] = lsum if first else l_ref[...] + lsum
            if half_last:
                acc1_ref[:, H2:] += jnp.dot(vt_last, pb_last[:, :H2], preferred_element_type=F32)
                acc2_ref[:, H2:] += jnp.dot(vt_last, pb_last[:, H2:], preferred_element_type=F32)
            return
        m_old = m_ref[...]
        cmax = jnp.max(s, axis=0, keepdims=True)
        if half_last:
            cmax = jnp.maximum(cmax, on_upper_half(jnp.max(s_last, axis=0, keepdims=True), NEG_BIG))
        m_new = jnp.maximum(m_old, cmax)
        alpha = jnp.exp2(m_old - m_new)
        p = jnp.exp2(s - m_new)
        lsum = jnp.sum(p, axis=0, keepdims=True)
        pb = p.astype(BF16)
        acc1_ref[...] = alpha[:, :T] * acc1_ref[...] + jnp.dot(vtb, pb[:, :T], preferred_element_type=F32)
        acc2_ref[...] = alpha[:, T:] * acc2_ref[...] + jnp.dot(vtb, pb[:, T:], preferred_element_type=F32)
        if half_last:
            m_half = jnp.concatenate([m_new[:, H2:T], m_new[:, T + H2:]], axis=1)
            p_last = jnp.exp2(s_last - m_half)
            lsum = lsum + on_upper_half(jnp.sum(p_last, axis=0, keepdims=True), 0.0)
            pb_last = p_last.astype(BF16)
            acc1_ref[:, H2:] += jnp.dot(vt_last, pb_last[:, :H2], preferred_element_type=F32)
            acc2_ref[:, H2:] += jnp.dot(vt_last, pb_last[:, H2:], preferred_element_type=F32)
        l_ref[...] = alpha * l_ref[...] + lsum
        m_ref[...] = m_new

    @pl.when(i > 0)
    def _():
        block(ratio * i - 2, ratio + 2, 0, plain=1, half_last=True, first=True)

    @pl.when(i == 0)
    def _():
        block(0, ratio, 1, half_last=True, first=True)

    n_far = jnp.maximum(ratio * i - 2, 0)

    def far_step(jj, carry):
        block(FAR_STEP * jj, FAR_STEP)
        return carry

    lax.fori_loop(0, n_far // FAR_STEP, far_step, 0)
    done = (n_far // FAR_STEP) * FAR_STEP
    size = FAR_STEP // 2
    while size >= 2:
        take = (n_far - done) >= size

        @pl.when(take)
        def _(done=done, size=size):
            block(done, size)

        done = done + jnp.where(take, size, 0)
        size //= 2

    lv = lamv_ref[...]
    lam = (jnp.exp(jnp.sum(lv[0:1] * lv[1:2], axis=1, keepdims=True))
           - jnp.exp(jnp.sum(lv[2:3] * lv[3:4], axis=1, keepdims=True)) + lam_init)
    l = l_ref[...]
    d = acc1_ref[...] / l[:, :T] - lam * (acc2_ref[...] / l[:, T:])
    ms = jnp.mean(d * d, axis=0, keepdims=True)
    y = d * lax.rsqrt(ms + EPS)
    o_ref[0] = (y.T * sg_ref[...]).astype(o_ref.dtype)


def _diff_attn(online, lam_init, q, k, vt, bias, lamv, sg):
    b, s, _ = q.shape
    T = Q_TILE
    TK = ATT_TILE
    nq = s // T
    return pl.pallas_call(
        functools.partial(_attn_body, lam_init, online),
        grid=(b, N_HEADS, nq),
        in_specs=[
            pl.BlockSpec(lamv.shape, lambda bi, h, i: (0, 0)),
            pl.BlockSpec((1, T, 2 * HEAD_DIM), lambda bi, h, i: (bi, i, h)),
            pl.BlockSpec((1, s, 2 * HEAD_DIM), lambda bi, h, i: (bi, 0, h)),
            pl.BlockSpec((1, 1, s // TK, 2 * HEAD_DIM, TK), lambda bi, h, i: (bi, h, 0, 0, 0)),
            pl.BlockSpec((bias.shape[0], 1, TK, T), lambda bi, h, i: (0, h, 0, 0),
                         pipeline_mode=pl.Buffered(1)),
            pl.BlockSpec(sg.shape, lambda bi, h, i: (0, 0)),
        ],
        out_specs=pl.BlockSpec((1, T, 2 * HEAD_DIM), lambda bi, h, i: (bi, i, h)),
        out_shape=jax.ShapeDtypeStruct((b, s, DIFF_WIDTH), BF16),
        scratch_shapes=[pltpu.VMEM((2 * HEAD_DIM, 2 * T), BF16), pltpu.VMEM((1, 2 * T), F32),
                        pltpu.VMEM((1, 2 * T), F32), pltpu.VMEM((2 * HEAD_DIM, T), F32),
                        pltpu.VMEM((2 * HEAD_DIM, T), F32)],
        compiler_params=pltpu.CompilerParams(dimension_semantics=("arbitrary", "arbitrary", "arbitrary"),
                                             vmem_limit_bytes=VMEM_LIMIT),
        name="diff_attn_online" if online else "diff_attn",
    )(lamv, q, k, vt, bias, sg)


def _rel_bucket_np(n):
    max_exact = NUM_BUCKETS // 2
    nf = np.maximum(n, 1).astype(np.float32)
    large = max_exact + (np.log(nf / max_exact) / math.log(MAX_DISTANCE / max_exact)
                         * (NUM_BUCKETS - max_exact)).astype(np.int32)
    large = np.minimum(large, NUM_BUCKETS - 1)
    return np.where(n < max_exact, n, large)


def _bias_body(kinds, tab_ref, idx_ref, o_ref):
    h = pl.program_id(0)
    for t, tile_kinds in enumerate(kinds):
        for i, row_kinds in enumerate(tile_kinds):
            for j, kind in enumerate(row_kinds):
                rows, cols = slice(i * LANES, (i + 1) * LANES), slice(j * LANES, (j + 1) * LANES)
                if kind != "band":
                    fill = NEG_BIG if kind == "masked" else 0.0
                    o_ref[t, 0, rows, cols] = jnp.full((LANES, LANES), fill, F32)
                    continue
                idx = idx_ref[t, rows, cols]
                acc = jnp.zeros(idx.shape, F32)
                for b in range(NUM_BUCKETS):
                    acc = jnp.where(idx == b, tab_ref[b, h], acc)
                o_ref[t, 0, rows, cols] = jnp.where(idx < 0, NEG_BIG, acc)


def _bias_tiles(tab):
    T, TK = Q_TILE, ATT_TILE
    n_tiles = T // TK + 1
    c = np.arange(TK)[:, None]
    r = np.arange(T)[None, :]
    dist = [r - c + TK * (1 - m) for m in range(n_tiles)]
    idx = jnp.asarray(np.stack([np.where(d >= 0, _rel_bucket_np(np.maximum(d, 0)), -1)
                                for d in dist]).astype(np.int32))

    def kind(d):
        return "masked" if (d < 0).all() else "zero" if (d >= MAX_DISTANCE).all() else "band"

    kinds = tuple(tuple(tuple(kind(d[i:i + LANES, j:j + LANES]) for j in range(0, T, LANES))
                        for i in range(0, TK, LANES)) for d in dist)
    return pl.pallas_call(
        functools.partial(_bias_body, kinds),
        grid=(N_HEADS,),
        in_specs=[pl.BlockSpec(memory_space=pltpu.SMEM),
                  pl.BlockSpec((n_tiles, TK, T), lambda h: (0, 0, 0))],
        out_specs=pl.BlockSpec((n_tiles, 1, TK, T), lambda h: (0, h, 0, 0)),
        out_shape=jax.ShapeDtypeStruct((n_tiles, N_HEADS, TK, T), F32),
        name="bias_tiles",
    )(tab, idx)


def kernel(x, rel_bias, ffn1_norm, ffn1_w_gate, ffn1_w_up, ffn1_w_down, mix_norm, w_in, conv_w, conv_b,
           conv_norm, q_norm, k_norm, lambda_q1, lambda_k1, lambda_q2, lambda_k2, subln_norm, w_out,
           ffn2_norm, ffn2_w_gate, ffn2_w_up, ffn2_w_down):
    B, S, _ = x.shape
    depth = w_in.shape[0]
    assert S % Q_TILE == 0 and Q_TILE == 2 * ATT_TILE and ROW_TILE == ATT_TILE and ATT_TILE >= MAX_DISTANCE

    grp = np.arange(GROUP_MM) // HEAD_DIM
    gsum = jnp.asarray(grp[:, None] == grp[None, :], BF16)
    tab = (rel_bias - rel_bias[NUM_BUCKETS - 1:]).astype(F32) * LOG2E
    bias = _bias_tiles(tab)
    row2 = lambda a: a.reshape(1, -1).astype(F32)

    xf = x.reshape(B * S, D_MODEL)
    for l in range(depth):
        lam_init = 0.8 - 0.6 * math.exp(-0.3 * l)
        qg = q_norm[l].astype(F32) * (LOG2E / math.sqrt(HEAD_DIM))
        kg = k_norm[l].astype(F32)
        qkg = jnp.concatenate([jnp.tile(qg, 2 * N_HEADS), jnp.tile(kg, 2 * N_HEADS)]).reshape(1, -1)
        score_bound = (HEAD_DIM * 1.02) * jnp.max(jnp.abs(qg)) * jnp.max(jnp.abs(kg)) + jnp.max(jnp.abs(tab))
        cw8 = jnp.broadcast_to(conv_w[l].astype(F32)[:, None, :], (CONV_K, SUBLANES, CONV_CH))
        x1, c, q, k, vt = _ffn_proj(xf, row2(ffn1_norm[l]), ffn1_w_gate[l].astype(BF16),
                                   ffn1_w_up[l].astype(BF16), ffn1_w_down[l].astype(BF16),
                                   row2(mix_norm[l]), w_in[l].astype(BF16), gsum, qkg, cw8,
                                   row2(conv_b[l]), row2(conv_norm[l]), S)
        lamv = jnp.stack([lambda_q1[l], lambda_k1[l], lambda_q2[l], lambda_k2[l]]).astype(F32)
        sg = row2(subln_norm[l]) * (1.0 - lam_init)
        att = lax.cond(score_bound <= SAFE_EXP2_BOUND,
                       functools.partial(_diff_attn, False, lam_init),
                       functools.partial(_diff_attn, True, lam_init),
                       q.reshape(B, S, DIFF_WIDTH), k.reshape(B, S, DIFF_WIDTH), vt, bias, lamv, sg)
        xf = _mix_ffn(x1, c, att.reshape(B * S, DIFF_WIDTH), w_out[l, :CONV_CH].astype(BF16),
                      w_out[l, CONV_CH:].astype(BF16), row2(ffn2_norm[l]), ffn2_w_gate[l].astype(BF16),
                      ffn2_w_up[l].astype(BF16), ffn2_w_down[l].astype(BF16))
    return xf.reshape(B, S, D_MODEL)
```

```python
import functools
import math

import jax
import jax.numpy as jnp
import numpy as np
from jax import lax
from jax.experimental import pallas as pl
from jax.experimental.pallas import tpu as pltpu

F32 = jnp.float32
BF16 = jnp.bfloat16

D_MODEL = 1024
CONV_CH = 512
CONV_K = 31
DIFF_WIDTH = 512
HEAD_DIM = 64
N_HEADS = 4
D_FF = 2816
NUM_BUCKETS = 32
MAX_DISTANCE = 128
EPS = 1e-6
LOG2E = 1.4426950408889634
NEG_BIG = -1e30
SAFE_EXP2_BOUND = 90.0

FF_CHUNK = 256
N_FF_CHUNKS = D_FF // FF_CHUNK
ROW_TILE = 512
MIX_TILE = 1024
ATT_TILE = 512
Q_TILE = 1024
FAR_STEP = 4
SIDE_WORK_LAG = 2
LANES = 128
SUBLANES = 8
CONV_HALO = 32
CONV_ROWS = 8
GROUP_MM = 256
VMEM_LIMIT = 56 * 1024 * 1024


def _rms(x, g):
    ms = jnp.mean(x * x, axis=-1, keepdims=True)
    return x * lax.rsqrt(ms + EPS) * g


def _zero_after(v):
    bits = pltpu.bitcast(v[0:1, :], jnp.uint32)
    return pltpu.bitcast((bits >> 16) >> 16, F32)


def _ffn(x, g_ref, wg_ref, wu_ref, wd_ref, act_ref, side_work=None):
    h = _rms(x, g_ref[...]).astype(BF16)
    zs = [None] * SIDE_WORK_LAG
    for c in range(N_FF_CHUNKS):
        lo = c * FF_CHUNK
        gate = jnp.dot(h, wg_ref[:, lo:lo + FF_CHUNK], preferred_element_type=F32)
        up = jnp.dot(h, wu_ref[:, lo:lo + FF_CHUNK], preferred_element_type=F32)
        z = zs.pop(0)
        if z is not None:
            up = up + z
        act_ref[:, lo:lo + FF_CHUNK] = (gate * jax.nn.sigmoid(gate) * up).astype(BF16)
        zs.append(side_work(c, gate) if side_work is not None else None)
    y = jnp.dot(act_ref[...], wd_ref[...], preferred_element_type=F32)
    return x + 0.5 * y


def _conv_prev_tile(ubuf, ush, cw_ref, cb_ref, cn_ref, c_ref):
    tm = c_ref.shape[0]
    base = CONV_HALO - (CONV_K - 1)
    n_pieces = N_FF_CHUNKS
    rows_per = -(-tm // (n_pieces * SUBLANES)) * SUBLANES

    def rows_after(r0, nr, z):
        groups = nr // SUBLANES
        bias = cb_ref[...] + z
        acc = jnp.broadcast_to(bias, (groups, SUBLANES, CONV_CH))
        for j in range(CONV_K):
            a, b = divmod(base + j, SUBLANES)
            lo = r0 + a * SUBLANES
            rows = ubuf[lo:lo + nr, :] if b == 0 else ush[b - 1, lo:lo + nr, :]
            acc = acc + cw_ref[j][None] * rows.reshape(groups, SUBLANES, CONV_CH)
        c = _rms(acc.reshape(nr, CONV_CH), cn_ref[...])
        c = c * jax.nn.sigmoid(c)
        c_ref[r0:r0 + nr, :] = c.astype(c_ref.dtype)
        return _zero_after(jnp.sum(c, axis=0, keepdims=True))

    def piece(p, gate):
        zg = _zero_after(gate)
        z = jnp.concatenate([zg] * (CONV_CH // FF_CHUNK), axis=1)
        r0 = p * rows_per
        end = min(r0 + rows_per, tm)
        while r0 < end:
            nr = min(CONV_ROWS, end - r0)
            z = rows_after(r0, nr, z)
            r0 += nr
        return z[:, :FF_CHUNK]

    return piece


def _ffn_proj_body(seq_tiles, x_ref, n1_ref, wg_ref, wu_ref, wd_ref, nm_ref, win_ref, gsum_ref, qkg_ref,
                   cw_ref, cb_ref, cn_ref, xo_ref, c_ref, q_ref, k_ref, vt_ref, act_ref, ubuf, ush):
    i = pl.program_id(0)
    tm = x_ref.shape[0]

    @pl.when(i == 0)
    def _():
        ubuf[...] = jnp.zeros(ubuf.shape, F32)
        ush[...] = jnp.zeros(ush.shape, F32)

    conv_piece = _conv_prev_tile(ubuf, ush, cw_ref, cb_ref, cn_ref, c_ref)
    x1 = _ffn(x_ref[...], n1_ref, wg_ref, wu_ref, wd_ref, act_ref, side_work=conv_piece)
    xo_ref[...] = x1
    h = _rms(x1, nm_ref[...]).astype(BF16)
    ag = jnp.dot(h, win_ref[:, 0:2 * CONV_CH], preferred_element_type=F32)
    u = ag[:, :CONV_CH] * jax.nn.sigmoid(ag[:, CONV_CH:])
    qk = jnp.dot(h, win_ref[:, 2 * CONV_CH:2 * CONV_CH + 2 * DIFF_WIDTH], preferred_element_type=F32)
    sq = (qk * qk).astype(BF16)
    ss = jnp.concatenate(
        [jnp.dot(sq[:, c * GROUP_MM:(c + 1) * GROUP_MM], gsum_ref[...], preferred_element_type=F32)
         for c in range(2 * DIFF_WIDTH // GROUP_MM)], axis=1)
    qkn = qk * lax.rsqrt(ss * (1.0 / HEAD_DIM) + EPS) * qkg_ref[...]
    q_ref[...] = qkn[:, :DIFF_WIDTH].astype(BF16)
    k_ref[...] = qkn[:, DIFF_WIDTH:].astype(BF16)
    v = jnp.dot(h, win_ref[:, 2 * CONV_CH + 2 * DIFF_WIDTH:], preferred_element_type=F32)
    vt_ref[0, :, 0] = v.T.reshape(N_HEADS, 2 * HEAD_DIM, tm).astype(BF16)

    tail = ubuf[tm:tm + CONV_HALO, :]
    ubuf[0:CONV_HALO, :] = jnp.where(i % seq_tiles == 0, 0.0, tail)
    ubuf[CONV_HALO:, :] = u
    for b in range(1, SUBLANES):
        ush[b - 1] = ubuf[b:b + ush.shape[1], :]


def _const_spec(shape):
    nd = len(shape)
    return pl.BlockSpec(shape, lambda i: (0,) * nd, pipeline_mode=pl.Buffered(1))


def _ffn_proj(x, n1, wg, wu, wd, nm, win, gsum, qkg, cw, cb, cn, seq):
    t = x.shape[0]
    tm = ROW_TILE
    n = t // tm
    cur = lambda w: pl.BlockSpec((tm, w), lambda i: (jnp.minimum(i, n - 1), 0))
    prev = lambda w: pl.BlockSpec((tm, w), lambda i: (jnp.maximum(i - 1, 0), 0))
    consts = (n1, wg, wu, wd, nm, win, gsum, qkg, cw, cb, cn)
    seq_tiles = seq // tm

    def vt_index(i):
        ii = jnp.minimum(i, n - 1)
        return (ii // seq_tiles, 0, ii % seq_tiles, 0, 0)

    return pl.pallas_call(
        functools.partial(_ffn_proj_body, seq_tiles),
        grid=(n + 1,),
        in_specs=[cur(D_MODEL)] + [_const_spec(a.shape) for a in consts],
        out_specs=[cur(D_MODEL), prev(CONV_CH), cur(DIFF_WIDTH), cur(DIFF_WIDTH),
                   pl.BlockSpec((1, N_HEADS, 1, 2 * HEAD_DIM, tm), vt_index)],
        out_shape=[jax.ShapeDtypeStruct((t, D_MODEL), F32), jax.ShapeDtypeStruct((t, CONV_CH), BF16),
                   jax.ShapeDtypeStruct((t, DIFF_WIDTH), BF16), jax.ShapeDtypeStruct((t, DIFF_WIDTH), BF16),
                   jax.ShapeDtypeStruct((t // seq, N_HEADS, seq_tiles, 2 * HEAD_DIM, tm), BF16)],
        scratch_shapes=[pltpu.VMEM((tm, D_FF), BF16), pltpu.VMEM((tm + CONV_HALO, CONV_CH), F32),
                        pltpu.VMEM((SUBLANES - 1, tm + CONV_HALO - SUBLANES, CONV_CH), F32)],
        compiler_params=pltpu.CompilerParams(dimension_semantics=("arbitrary",),
                                             vmem_limit_bytes=VMEM_LIMIT),
        name="ffn_proj",
    )(x, *consts)


def _mix_ffn_body(x_ref, c_ref, att_ref, woc_ref, wod_ref, n2_ref, wg_ref, wu_ref, wd_ref, o_ref,
                  act_ref):
    mix = (jnp.dot(c_ref[...], woc_ref[...], preferred_element_type=F32)
           + jnp.dot(att_ref[...], wod_ref[...], preferred_element_type=F32))
    x1 = x_ref[...] + mix
    o_ref[...] = _ffn(x1, n2_ref, wg_ref, wu_ref, wd_ref, act_ref)


def _mix_ffn(x, c, att, woc, wod, n2, wg, wu, wd):
    t = x.shape[0]
    tm = MIX_TILE
    row = lambda w: pl.BlockSpec((tm, w), lambda i: (i, 0))
    consts = (woc, wod, n2, wg, wu, wd)
    return pl.pallas_call(
        _mix_ffn_body,
        grid=(t // tm,),
        in_specs=[row(D_MODEL), row(CONV_CH), row(DIFF_WIDTH)] + [_const_spec(a.shape) for a in consts],
        out_specs=row(D_MODEL),
        out_shape=jax.ShapeDtypeStruct((t, D_MODEL), F32),
        scratch_shapes=[pltpu.VMEM((tm, D_FF), BF16)],
        compiler_params=pltpu.CompilerParams(dimension_semantics=("arbitrary",),
                                             vmem_limit_bytes=VMEM_LIMIT),
        name="mix_ffn",
    )(x, c, att, *consts)


def _attn_body(lam_init, online, lamv_ref, q_ref, k_ref, vt_ref, bias_ref, sg_ref, o_ref,
               qbd_ref, m_ref, l_ref, acc1_ref, acc2_ref):
    T = Q_TILE
    TK = ATT_TILE
    ratio = T // TK
    i = pl.program_id(2)

    qt = q_ref[0].astype(F32).T
    row = lax.broadcasted_iota(jnp.int32, qt.shape, 0)
    qbd_ref[:, :T] = jnp.where(row < HEAD_DIM, qt, 0.0).astype(BF16)
    qbd_ref[:, T:] = jnp.where(row >= HEAD_DIM, qt, 0.0).astype(BF16)
    if online:
        m_ref[...] = jnp.full(m_ref.shape, NEG_BIG, F32)
        l_ref[...] = jnp.zeros(l_ref.shape, F32)
        acc1_ref[...] = jnp.zeros(acc1_ref.shape, F32)
        acc2_ref[...] = jnp.zeros(acc2_ref.shape, F32)

    H2 = T // 2

    def on_upper_half(v, fill):
        pad = jnp.full((1, H2), fill, F32)
        return jnp.concatenate([pad, v[:, :H2], pad, v[:, H2:]], axis=1)

    def block(j, nblk, first_bias=None, plain=0, half_last=False, first=False):
        n_main = nblk - 1 if half_last else nblk
        kb = k_ref[0, pl.ds(pl.multiple_of(j * TK, TK), nblk * TK), :]
        s = jnp.dot(kb[:n_main * TK], qbd_ref[...], preferred_element_type=F32)
        if first_bias is not None:
            nb = n_main - plain
            b = bias_ref[first_bias:first_bias + nb, 0].reshape(nb * TK, T)
            biased = s[plain * TK:] + jnp.concatenate([b, b], axis=1)
            s = biased if plain == 0 else jnp.concatenate([s[:plain * TK], biased], axis=0)
        vtb = jnp.concatenate([vt_ref[0, 0, j + t] for t in range(n_main)], axis=1)
        if half_last:
            q_half = jnp.concatenate([qbd_ref[:, H2:T], qbd_ref[:, T + H2:]], axis=1)
            b_last = bias_ref[first_bias + n_main - plain, 0][:, H2:]
            s_last = (jnp.dot(kb[n_main * TK:], q_half, preferred_element_type=F32)
                      + jnp.concatenate([b_last, b_last], axis=1))
            vt_last = vt_ref[0, 0, j + n_main]
        if not online:
            p = jnp.exp2(s)
            lsum = jnp.sum(p, axis=0, keepdims=True)
            pb = p.astype(BF16)
            pv1 = jnp.dot(vtb, pb[:, :T], preferred_element_type=F32)
            pv2 = jnp.dot(vtb, pb[:, T:], preferred_element_type=F32)
            if half_last:
                p_last = jnp.exp2(s_last)
                lsum = lsum + on_upper_half(jnp.sum(p_last, axis=0, keepdims=True), 0.0)
                pb_last = p_last.astype(BF16)
            acc1_ref[...] = pv1 if first else acc1_ref[...] + pv1
            acc2_ref[...] = pv2 if first else acc2_ref[...] + pv2
            l_ref[...] = lsum if first else l_ref[...] + lsum
            if half_last:
                acc1_ref[:, H2:] += jnp.dot(vt_last, pb_last[:, :H2], preferred_element_type=F32)
                acc2_ref[:, H2:] += jnp.dot(vt_last, pb_last[:, H2:], preferred_element_type=F32)
            return
        m_old = m_ref[...]
        cmax = jnp.max(s, axis=0, keepdims=True)
        if half_last:
            cmax = jnp.maximum(cmax, on_upper_half(jnp.max(s_last, axis=0, keepdims=True), NEG_BIG))
        m_new = jnp.maximum(m_old, cmax)
        alpha = jnp.exp2(m_old - m_new)
        p = jnp.exp2(s - m_new)
        lsum = jnp.sum(p, axis=0, keepdims=True)
        pb = p.astype(BF16)
        acc1_ref[...] = alpha[:, :T] * acc1_ref[...] + jnp.dot(vtb, pb[:, :T], preferred_element_type=F32)
        acc2_ref[...] = alpha[:, T:] * acc2_ref[...] + jnp.dot(vtb, pb[:, T:], preferred_element_type=F32)
        if half_last:
            m_half = jnp.concatenate([m_new[:, H2:T], m_new[:, T + H2:]], axis=1)
            p_last = jnp.exp2(s_last - m_half)
            lsum = lsum + on_upper_half(jnp.sum(p_last, axis=0, keepdims=True), 0.0)
            pb_last = p_last.astype(BF16)
            acc1_ref[:, H2:] += jnp.dot(vt_last, pb_last[:, :H2], preferred_element_type=F32)
            acc2_ref[:, H2:] += jnp.dot(vt_last, pb_last[:, H2:], preferred_element_type=F32)
        l_ref[...] = alpha * l_ref[...] + lsum
        m_ref[...] = m_new

    @pl.when(i > 0)
    def _():
        block(ratio * i - 2, ratio + 2, 0, plain=1, half_last=True, first=True)

    @pl.when(i == 0)
    def _():
        block(0, ratio, 1, half_last=True, first=True)

    n_far = jnp.maximum(ratio * i - 2, 0)

    def far_step(jj, carry):
        block(FAR_STEP * jj, FAR_STEP)
        return carry

    lax.fori_loop(0, n_far // FAR_STEP, far_step, 0)
    done = (n_far // FAR_STEP) * FAR_STEP
    size = FAR_STEP // 2
    while size >= 2:
        take = (n_far - done) >= size

        @pl.when(take)
        def _(done=done, size=size):
            block(done, size)

        done = done + jnp.where(take, size, 0)
        size //= 2

    lv = lamv_ref[...]
    lam = (jnp.exp(jnp.sum(lv[0:1] * lv[1:2], axis=1, keepdims=True))
           - jnp.exp(jnp.sum(lv[2:3] * lv[3:4], axis=1, keepdims=True)) + lam_init)
    l = l_ref[...]
    d = acc1_ref[...] / l[:, :T] - lam * (acc2_ref[...] / l[:, T:])
    ms = jnp.mean(d * d, axis=0, keepdims=True)
    y = d * lax.rsqrt(ms + EPS)
    o_ref[0] = (y.T * sg_ref[...]).astype(o_ref.dtype)


def _diff_attn(online, lam_init, q, k, vt, bias, lamv, sg):
    b, s, _ = q.shape
    T = Q_TILE
    TK = ATT_TILE
    nq = s // T
    return pl.pallas_call(
        functools.partial(_attn_body, lam_init, online),
        grid=(b, N_HEADS, nq),
        in_specs=[
            pl.BlockSpec(lamv.shape, lambda bi, h, i: (0, 0)),
            pl.BlockSpec((1, T, 2 * HEAD_DIM), lambda bi, h, i: (bi, i, h)),
            pl.BlockSpec((1, s, 2 * HEAD_DIM), lambda bi, h, i: (bi, 0, h)),
            pl.BlockSpec((1, 1, s // TK, 2 * HEAD_DIM, TK), lambda bi, h, i: (bi, h, 0, 0, 0)),
            pl.BlockSpec((bias.shape[0], 1, TK, T), lambda bi, h, i: (0, h, 0, 0),
                         pipeline_mode=pl.Buffered(1)),
            pl.BlockSpec(sg.shape, lambda bi, h, i: (0, 0)),
        ],
        out_specs=pl.BlockSpec((1, T, 2 * HEAD_DIM), lambda bi, h, i: (bi, i, h)),
        out_shape=jax.ShapeDtypeStruct((b, s, DIFF_WIDTH), BF16),
        scratch_shapes=[pltpu.VMEM((2 * HEAD_DIM, 2 * T), BF16), pltpu.VMEM((1, 2 * T), F32),
                        pltpu.VMEM((1, 2 * T), F32), pltpu.VMEM((2 * HEAD_DIM, T), F32),
                        pltpu.VMEM((2 * HEAD_DIM, T), F32)],
        compiler_params=pltpu.CompilerParams(dimension_semantics=("arbitrary", "arbitrary", "arbitrary"),
                                             vmem_limit_bytes=VMEM_LIMIT),
        name="diff_attn_online" if online else "diff_attn",
    )(lamv, q, k, vt, bias, sg)


def _rel_bucket_np(n):
    max_exact = NUM_BUCKETS // 2
    nf = np.maximum(n, 1).astype(np.float32)
    large = max_exact + (np.log(nf / max_exact) / math.log(MAX_DISTANCE / max_exact)
                         * (NUM_BUCKETS - max_exact)).astype(np.int32)
    large = np.minimum(large, NUM_BUCKETS - 1)
    return np.where(n < max_exact, n, large)


def _bias_body(kinds, tab_ref, idx_ref, o_ref):
    h = pl.program_id(0)
    for t, tile_kinds in enumerate(kinds):
        for i, row_kinds in enumerate(tile_kinds):
            for j, kind in enumerate(row_kinds):
                rows, cols = slice(i * LANES, (i + 1) * LANES), slice(j * LANES, (j + 1) * LANES)
                if kind != "band":
                    fill = NEG_BIG if kind == "masked" else 0.0
                    o_ref[t, 0, rows, cols] = jnp.full((LANES, LANES), fill, F32)
                    continue
                idx = idx_ref[t, rows, cols]
                acc = jnp.zeros(idx.shape, F32)
                for b in range(NUM_BUCKETS):
                    acc = jnp.where(idx == b, tab_ref[b, h], acc)
                o_ref[t, 0, rows, cols] = jnp.where(idx < 0, NEG_BIG, acc)


def _bias_tiles(tab):
    T, TK = Q_TILE, ATT_TILE
    n_tiles = T // TK + 1
    c = np.arange(TK)[:, None]
    r = np.arange(T)[None, :]
    dist = [r - c + TK * (1 - m) for m in range(n_tiles)]
    idx = jnp.asarray(np.stack([np.where(d >= 0, _rel_bucket_np(np.maximum(d, 0)), -1)
                                for d in dist]).astype(np.int32))

    def kind(d):
        return "masked" if (d < 0).all() else "zero" if (d >= MAX_DISTANCE).all() else "band"

    kinds = tuple(tuple(tuple(kind(d[i:i + LANES, j:j + LANES]) for j in range(0, T, LANES))
                        for i in range(0, TK, LANES)) for d in dist)
    return pl.pallas_call(
        functools.partial(_bias_body, kinds),
        grid=(N_HEADS,),
        in_specs=[pl.BlockSpec(memory_space=pltpu.SMEM),
                  pl.BlockSpec((n_tiles, TK, T), lambda h: (0, 0, 0))],
        out_specs=pl.BlockSpec((n_tiles, 1, TK, T), lambda h: (0, h, 0, 0)),
        out_shape=jax.ShapeDtypeStruct((n_tiles, N_HEADS, TK, T), F32),
        name="bias_tiles",
    )(tab, idx)


def kernel(x, rel_bias, ffn1_norm, ffn1_w_gate, ffn1_w_up, ffn1_w_down, mix_norm, w_in, conv_w, conv_b,
           conv_norm, q_norm, k_norm, lambda_q1, lambda_k1, lambda_q2, lambda_k2, subln_norm, w_out,
           ffn2_norm, ffn2_w_gate, ffn2_w_up, ffn2_w_down):
    B, S, _ = x.shape
    depth = w_in.shape[0]
    assert S % Q_TILE == 0 and Q_TILE == 2 * ATT_TILE and ROW_TILE == ATT_TILE and ATT_TILE >= MAX_DISTANCE

    grp = np.arange(GROUP_MM) // HEAD_DIM
    gsum = jnp.asarray(grp[:, None] == grp[None, :], BF16)
    tab = (rel_bias - rel_bias[NUM_BUCKETS - 1:]).astype(F32) * LOG2E
    bias = _bias_tiles(tab)
    row2 = lambda a: a.reshape(1, -1).astype(F32)

    xf = x.reshape(B * S, D_MODEL)
    for l in range(depth):
        lam_init = 0.8 - 0.6 * math.exp(-0.3 * l)
        qg = q_norm[l].astype(F32) * (LOG2E / math.sqrt(HEAD_DIM))
        kg = k_norm[l].astype(F32)
        qkg = jnp.concatenate([jnp.tile(qg, 2 * N_HEADS), jnp.tile(kg, 2 * N_HEADS)]).reshape(1, -1)
        score_bound = (HEAD_DIM * 1.02) * jnp.max(jnp.abs(qg)) * jnp.max(jnp.abs(kg)) + jnp.max(jnp.abs(tab))
        cw8 = jnp.broadcast_to(conv_w[l].astype(F32)[:, None, :], (CONV_K, SUBLANES, CONV_CH))
        x1, c, q, k, vt = _ffn_proj(xf, row2(ffn1_norm[l]), ffn1_w_gate[l].astype(BF16),
                                   ffn1_w_up[l].astype(BF16), ffn1_w_down[l].astype(BF16),
                                   row2(mix_norm[l]), w_in[l].astype(BF16), gsum, qkg, cw8,
                                   row2(conv_b[l]), row2(conv_norm[l]), S)
        lamv = jnp.stack([lambda_q1[l], lambda_k1[l], lambda_q2[l], lambda_k2[l]]).astype(F32)
        sg = row2(subln_norm[l]) * (1.0 - lam_init)
        att = lax.cond(score_bound <= SAFE_EXP2_BOUND,
                       functools.partial(_diff_attn, False, lam_init),
                       functools.partial(_diff_attn, True, lam_init),
                       q.reshape(B, S, DIFF_WIDTH), k.reshape(B, S, DIFF_WIDTH), vt, bias, lamv, sg)
        xf = _mix_ffn(x1, c, att.reshape(B * S, DIFF_WIDTH), w_out[l, :CONV_CH].astype(BF16),
                      w_out[l, CONV_CH:].astype(BF16), row2(ffn2_norm[l]), ffn2_w_gate[l].astype(BF16),
                      ffn2_w_up[l].astype(BF16), ffn2_w_down[l].astype(BF16))
    return xf.reshape(B, S, D_MODEL)
```

```python
import functools
import math

import jax
import jax.numpy as jnp
import numpy as np
from jax import lax
from jax.experimental import pallas as pl
from jax.experimental.pallas import tpu as pltpu

F32 = jnp.float32
BF16 = jnp.bfloat16

D_MODEL = 1024
CONV_CH = 512
CONV_K = 31
DIFF_WIDTH = 512
HEAD_DIM = 64
N_HEADS = 4
D_FF = 2816
NUM_BUCKETS = 32
MAX_DISTANCE = 128
EPS = 1e-6
LOG2E = 1.4426950408889634
NEG_BIG = -1e30
SAFE_EXP2_BOUND = 90.0

FF_CHUNK = 256
N_FF_CHUNKS = D_FF // FF_CHUNK
ROW_TILE = 512
MIX_TILE = 1024
ATT_TILE = 512
Q_TILE = 1024
FAR_STEP = 4
SIDE_WORK_LAG = 2
LANES = 128
SUBLANES = 8
CONV_HALO = 32
CONV_ROWS = 16
GROUP_MM = 256
VMEM_LIMIT = 56 * 1024 * 1024


def _rms(x, g):
    ms = jnp.mean(x * x, axis=-1, keepdims=True)
    return x * lax.rsqrt(ms + EPS) * g


def _zero_after(v):
    bits = pltpu.bitcast(v[0:1, :], jnp.uint32)
    return pltpu.bitcast((bits >> 16) >> 16, F32)


def _ffn(x, g_ref, wg_ref, wu_ref, wd_ref, act_ref, side_work=None):
    h = _rms(x, g_ref[...]).astype(BF16)
    zs = [None] * SIDE_WORK_LAG
    for c in range(N_FF_CHUNKS):
        lo = c * FF_CHUNK
        gate = jnp.dot(h, wg_ref[:, lo:lo + FF_CHUNK], preferred_element_type=F32)
        up = jnp.dot(h, wu_ref[:, lo:lo + FF_CHUNK], preferred_element_type=F32)
        z = zs.pop(0)
        if z is not None:
            up = up + z
        act_ref[:, lo:lo + FF_CHUNK] = (gate * jax.nn.sigmoid(gate) * up).astype(BF16)
        zs.append(side_work(c, gate) if side_work is not None else None)
    y = jnp.dot(act_ref[...], wd_ref[...], preferred_element_type=F32)
    return x + 0.5 * y


def _conv_prev_tile(ubuf, ush, cw_ref, cb_ref, cn_ref, c_ref):
    tm = c_ref.shape[0]
    base = CONV_HALO - (CONV_K - 1)
    n_pieces = N_FF_CHUNKS
    rows_per = -(-tm // (n_pieces * SUBLANES)) * SUBLANES

    def rows_after(r0, nr, z):
        groups = nr // SUBLANES
        bias = cb_ref[...] + z
        acc = jnp.broadcast_to(bias, (groups, SUBLANES, CONV_CH))
        for j in range(CONV_K):
            a, b = divmod(base + j, SUBLANES)
            lo = r0 + a * SUBLANES
            rows = ubuf[lo:lo + nr, :] if b == 0 else ush[b - 1, lo:lo + nr, :]
            acc = acc + cw_ref[j][None] * rows.reshape(groups, SUBLANES, CONV_CH)
        c = _rms(acc.reshape(nr, CONV_CH), cn_ref[...])
        c = c * jax.nn.sigmoid(c)
        c_ref[r0:r0 + nr, :] = c.astype(c_ref.dtype)
        return _zero_after(jnp.sum(c, axis=0, keepdims=True))

    def piece(p, gate):
        zg = _zero_after(gate)
        z = jnp.concatenate([zg] * (CONV_CH // FF_CHUNK), axis=1)
        r0 = p * rows_per
        end = min(r0 + rows_per, tm)
        while r0 < end:
            nr = min(CONV_ROWS, end - r0)
            z = rows_after(r0, nr, z)
            r0 += nr
        return z[:, :FF_CHUNK]

    return piece


def _ffn_proj_body(seq_tiles, x_ref, n1_ref, wg_ref, wu_ref, wd_ref, nm_ref, win_ref, gsum_ref, qkg_ref,
                   cw_ref, cb_ref, cn_ref, xo_ref, c_ref, q_ref, k_ref, vt_ref, act_ref, ubuf, ush):
    i = pl.program_id(0)
    tm = x_ref.shape[0]

    @pl.when(i == 0)
    def _():
        ubuf[...] = jnp.zeros(ubuf.shape, F32)
        ush[...] = jnp.zeros(ush.shape, F32)

    conv_piece = _conv_prev_tile(ubuf, ush, cw_ref, cb_ref, cn_ref, c_ref)
    x1 = _ffn(x_ref[...], n1_ref, wg_ref, wu_ref, wd_ref, act_ref, side_work=conv_piece)
    xo_ref[...] = x1
    h = _rms(x1, nm_ref[...]).astype(BF16)
    ag = jnp.dot(h, win_ref[:, 0:2 * CONV_CH], preferred_element_type=F32)
    u = ag[:, :CONV_CH] * jax.nn.sigmoid(ag[:, CONV_CH:])
    qk = jnp.dot(h, win_ref[:, 2 * CONV_CH:2 * CONV_CH + 2 * DIFF_WIDTH], preferred_element_type=F32)
    sq = (qk * qk).astype(BF16)
    ss = jnp.concatenate(
        [jnp.dot(sq[:, c * GROUP_MM:(c + 1) * GROUP_MM], gsum_ref[...], preferred_element_type=F32)
         for c in range(2 * DIFF_WIDTH // GROUP_MM)], axis=1)
    qkn = qk * lax.rsqrt(ss * (1.0 / HEAD_DIM) + EPS) * qkg_ref[...]
    q_ref[...] = qkn[:, :DIFF_WIDTH].astype(BF16)
    k_ref[...] = qkn[:, DIFF_WIDTH:].astype(BF16)
    v = jnp.dot(h, win_ref[:, 2 * CONV_CH + 2 * DIFF_WIDTH:], preferred_element_type=F32)
    vt_ref[0, :, 0] = v.T.reshape(N_HEADS, 2 * HEAD_DIM, tm).astype(BF16)

    tail = ubuf[tm:tm + CONV_HALO, :]
    ubuf[0:CONV_HALO, :] = jnp.where(i % seq_tiles == 0, 0.0, tail)
    ubuf[CONV_HALO:, :] = u
    for b in range(1, SUBLANES):
        ush[b - 1] = ubuf[b:b + ush.shape[1], :]


def _const_spec(shape):
    nd = len(shape)
    return pl.BlockSpec(shape, lambda i: (0,) * nd, pipeline_mode=pl.Buffered(1))


def _ffn_proj(x, n1, wg, wu, wd, nm, win, gsum, qkg, cw, cb, cn, seq):
    t = x.shape[0]
    tm = ROW_TILE
    n = t // tm
    cur = lambda w: pl.BlockSpec((tm, w), lambda i: (jnp.minimum(i, n - 1), 0))
    prev = lambda w: pl.BlockSpec((tm, w), lambda i: (jnp.maximum(i - 1, 0), 0))
    consts = (n1, wg, wu, wd, nm, win, gsum, qkg, cw, cb, cn)
    seq_tiles = seq // tm

    def vt_index(i):
        ii = jnp.minimum(i, n - 1)
        return (ii // seq_tiles, 0, ii % seq_tiles, 0, 0)

    return pl.pallas_call(
        functools.partial(_ffn_proj_body, seq_tiles),
        grid=(n + 1,),
        in_specs=[cur(D_MODEL)] + [_const_spec(a.shape) for a in consts],
        out_specs=[cur(D_MODEL), prev(CONV_CH), cur(DIFF_WIDTH), cur(DIFF_WIDTH),
                   pl.BlockSpec((1, N_HEADS, 1, 2 * HEAD_DIM, tm), vt_index)],
        out_shape=[jax.ShapeDtypeStruct((t, D_MODEL), F32), jax.ShapeDtypeStruct((t, CONV_CH), BF16),
                   jax.ShapeDtypeStruct((t, DIFF_WIDTH), BF16), jax.ShapeDtypeStruct((t, DIFF_WIDTH), BF16),
                   jax.ShapeDtypeStruct((t // seq, N_HEADS, seq_tiles, 2 * HEAD_DIM, tm), BF16)],
        scratch_shapes=[pltpu.VMEM((tm, D_FF), BF16), pltpu.VMEM((tm + CONV_HALO, CONV_CH), F32),
                        pltpu.VMEM((SUBLANES - 1, tm + CONV_HALO - SUBLANES, CONV_CH), F32)],
        compiler_params=pltpu.CompilerParams(dimension_semantics=("arbitrary",),
                                             vmem_limit_bytes=VMEM_LIMIT),
        name="ffn_proj",
    )(x, *consts)


def _mix_ffn_body(x_ref, c_ref, att_ref, woc_ref, wod_ref, n2_ref, wg_ref, wu_ref, wd_ref, o_ref,
                  act_ref):
    mix = (jnp.dot(c_ref[...], woc_ref[...], preferred_element_type=F32)
           + jnp.dot(att_ref[...], wod_ref[...], preferred_element_type=F32))
    x1 = x_ref[...] + mix
    o_ref[...] = _ffn(x1, n2_ref, wg_ref, wu_ref, wd_ref, act_ref)


def _mix_ffn(x, c, att, woc, wod, n2, wg, wu, wd):
    t = x.shape[0]
    tm = MIX_TILE
    row = lambda w: pl.BlockSpec((tm, w), lambda i: (i, 0))
    consts = (woc, wod, n2, wg, wu, wd)
    return pl.pallas_call(
        _mix_ffn_body,
        grid=(t // tm,),
        in_specs=[row(D_MODEL), row(CONV_CH), row(DIFF_WIDTH)] + [_const_spec(a.shape) for a in consts],
        out_specs=row(D_MODEL),
        out_shape=jax.ShapeDtypeStruct((t, D_MODEL), F32),
        scratch_shapes=[pltpu.VMEM((tm, D_FF), BF16)],
        compiler_params=pltpu.CompilerParams(dimension_semantics=("arbitrary",),
                                             vmem_limit_bytes=VMEM_LIMIT),
        name="mix_ffn",
    )(x, c, att, *consts)


def _attn_body(lam_init, online, lamv_ref, q_ref, k_ref, vt_ref, bias_ref, sg_ref, o_ref,
               qbd_ref, m_ref, l_ref, acc1_ref, acc2_ref):
    T = Q_TILE
    TK = ATT_TILE
    ratio = T // TK
    i = pl.program_id(2)

    qt = q_ref[0].astype(F32).T
    row = lax.broadcasted_iota(jnp.int32, qt.shape, 0)
    qbd_ref[:, :T] = jnp.where(row < HEAD_DIM, qt, 0.0).astype(BF16)
    qbd_ref[:, T:] = jnp.where(row >= HEAD_DIM, qt, 0.0).astype(BF16)
    if online:
        m_ref[...] = jnp.full(m_ref.shape, NEG_BIG, F32)
        l_ref[...] = jnp.zeros(l_ref.shape, F32)
        acc1_ref[...] = jnp.zeros(acc1_ref.shape, F32)
        acc2_ref[...] = jnp.zeros(acc2_ref.shape, F32)

    H2 = T // 2

    def on_upper_half(v, fill):
        pad = jnp.full((1, H2), fill, F32)
        return jnp.concatenate([pad, v[:, :H2], pad, v[:, H2:]], axis=1)

    def block(j, nblk, first_bias=None, plain=0, half_last=False, first=False):
        n_main = nblk - 1 if half_last else nblk
        kb = k_ref[0, pl.ds(pl.multiple_of(j * TK, TK), nblk * TK), :]
        s = jnp.dot(kb[:n_main * TK], qbd_ref[...], preferred_element_type=F32)
        if first_bias is not None:
            nb = n_main - plain
            b = bias_ref[first_bias:first_bias + nb, 0].reshape(nb * TK, T)
            biased = s[plain * TK:] + jnp.concatenate([b, b], axis=1)
            s = biased if plain == 0 else jnp.concatenate([s[:plain * TK], biased], axis=0)
        vtb = jnp.concatenate([vt_ref[0, 0, j + t] for t in range(n_main)], axis=1)
        if half_last:
            q_half = jnp.concatenate([qbd_ref[:, H2:T], qbd_ref[:, T + H2:]], axis=1)
            b_last = bias_ref[first_bias + n_main - plain, 0][:, H2:]
            s_last = (jnp.dot(kb[n_main * TK:], q_half, preferred_element_type=F32)
                      + jnp.concatenate([b_last, b_last], axis=1))
            vt_last = vt_ref[0, 0, j + n_main]
        if not online:
            p = jnp.exp2(s)
            lsum = jnp.sum(p, axis=0, keepdims=True)
            pb = p.astype(BF16)
            pv1 = jnp.dot(vtb, pb[:, :T], preferred_element_type=F32)
            pv2 = jnp.dot(vtb, pb[:, T:], preferred_element_type=F32)
            if half_last:
                p_last = jnp.exp2(s_last)
                lsum = lsum + on_upper_half(jnp.sum(p_last, axis=0, keepdims=True), 0.0)
                pb_last = p_last.astype(BF16)
            acc1_ref[...] = pv1 if first else acc1_ref[...] + pv1
            acc2_ref[...] = pv2 if first else acc2_ref[...] + pv2
            l_ref[...] = lsum if first else l_ref[...] + lsum
            if half_last:
                acc1_ref[:, H2:] += jnp.dot(vt_last, pb_last[:, :H2], preferred_element_type=F32)
                acc2_ref[:, H2:] += jnp.dot(vt_last, pb_last[:, H2:], preferred_element_type=F32)
            return
        m_old = m_ref[...]
        cmax = jnp.max(s, axis=0, keepdims=True)
        if half_last:
            cmax = jnp.maximum(cmax, on_upper_half(jnp.max(s_last, axis=0, keepdims=True), NEG_BIG))
        m_new = jnp.maximum(m_old, cmax)
        alpha = jnp.exp2(m_old - m_new)
        p = jnp.exp2(s - m_new)
        lsum = jnp.sum(p, axis=0, keepdims=True)
        pb = p.astype(BF16)
        acc1_ref[...] = alpha[:, :T] * acc1_ref[...] + jnp.dot(vtb, pb[:, :T], preferred_element_type=F32)
        acc2_ref[...] = alpha[:, T:] * acc2_ref[...] + jnp.dot(vtb, pb[:, T:], preferred_element_type=F32)
        if half_last:
            m_half = jnp.concatenate([m_new[:, H2:T], m_new[:, T + H2:]], axis=1)
            p_last = jnp.exp2(s_last - m_half)
            lsum = lsum + on_upper_half(jnp.sum(p_last, axis=0, keepdims=True), 0.0)
            pb_last = p_last.astype(BF16)
            acc1_ref[:, H2:] += jnp.dot(vt_last, pb_last[:, :H2], preferred_element_type=F32)
            acc2_ref[:, H2:] += jnp.dot(vt_last, pb_last[:, H2:], preferred_element_type=F32)
        l_ref[...] = alpha * l_ref[...] + lsum
        m_ref[...] = m_new

    @pl.when(i > 0)
    def _():
        block(ratio * i - 2, ratio + 2, 0, plain=1, half_last=True, first=True)

    @pl.when(i == 0)
    def _():
        block(0, ratio, 1, half_last=True, first=True)

    n_far = jnp.maximum(ratio * i - 2, 0)

    def far_step(jj, carry):
        block(FAR_STEP * jj, FAR_STEP)
        return carry

    lax.fori_loop(0, n_far // FAR_STEP, far_step, 0)
    done = (n_far // FAR_STEP) * FAR_STEP
    size = FAR_STEP // 2
    while size >= 2:
        take = (n_far - done) >= size

        @pl.when(take)
        def _(done=done, size=size):
            block(done, size)

        done = done + jnp.where(take, size, 0)
        size //= 2

    lv = lamv_ref[...]
    lam = (jnp.exp(jnp.sum(lv[0:1] * lv[1:2], axis=1, keepdims=True))
           - jnp.exp(jnp.sum(lv[2:3] * lv[3:4], axis=1, keepdims=True)) + lam_init)
    l = l_ref[...]
    d = acc1_ref[...] / l[:, :T] - lam * (acc2_ref[...] / l[:, T:])
    ms = jnp.mean(d * d, axis=0, keepdims=True)
    y = d * lax.rsqrt(ms + EPS)
    o_ref[0] = (y.T * sg_ref[...]).astype(o_ref.dtype)


def _diff_attn(online, lam_init, q, k, vt, bias, lamv, sg):
    b, s, _ = q.shape
    T = Q_TILE
    TK = ATT_TILE
    nq = s // T
    return pl.pallas_call(
        functools.partial(_attn_body, lam_init, online),
        grid=(b, N_HEADS, nq),
        in_specs=[
            pl.BlockSpec(lamv.shape, lambda bi, h, i: (0, 0)),
            pl.BlockSpec((1, T, 2 * HEAD_DIM), lambda bi, h, i: (bi, i, h)),
            pl.BlockSpec((1, s, 2 * HEAD_DIM), lambda bi, h, i: (bi, 0, h)),
            pl.BlockSpec((1, 1, s // TK, 2 * HEAD_DIM, TK), lambda bi, h, i: (bi, h, 0, 0, 0)),
            pl.BlockSpec((bias.shape[0], 1, TK, T), lambda bi, h, i: (0, h, 0, 0)),
            pl.BlockSpec(sg.shape, lambda bi, h, i: (0, 0)),
        ],
        out_specs=pl.BlockSpec((1, T, 2 * HEAD_DIM), lambda bi, h, i: (bi, i, h)),
        out_shape=jax.ShapeDtypeStruct((b, s, DIFF_WIDTH), BF16),
        scratch_shapes=[pltpu.VMEM((2 * HEAD_DIM, 2 * T), BF16), pltpu.VMEM((1, 2 * T), F32),
                        pltpu.VMEM((1, 2 * T), F32), pltpu.VMEM((2 * HEAD_DIM, T), F32),
                        pltpu.VMEM((2 * HEAD_DIM, T), F32)],
        compiler_params=pltpu.CompilerParams(dimension_semantics=("arbitrary", "arbitrary", "arbitrary"),
                                             vmem_limit_bytes=VMEM_LIMIT),
        name="diff_attn_online" if online else "diff_attn",
    )(lamv, q, k, vt, bias, sg)


def _rel_bucket_np(n):
    max_exact = NUM_BUCKETS // 2
    nf = np.maximum(n, 1).astype(np.float32)
    large = max_exact + (np.log(nf / max_exact) / math.log(MAX_DISTANCE / max_exact)
                         * (NUM_BUCKETS - max_exact)).astype(np.int32)
    large = np.minimum(large, NUM_BUCKETS - 1)
    return np.where(n < max_exact, n, large)


def _bias_body(kinds, tab_ref, idx_ref, o_ref):
    h = pl.program_id(0)
    for t, tile_kinds in enumerate(kinds):
        for i, row_kinds in enumerate(tile_kinds):
            for j, kind in enumerate(row_kinds):
                rows, cols = slice(i * LANES, (i + 1) * LANES), slice(j * LANES, (j + 1) * LANES)
                if kind != "band":
                    fill = NEG_BIG if kind == "masked" else 0.0
                    o_ref[t, 0, rows, cols] = jnp.full((LANES, LANES), fill, F32)
                    continue
                idx = idx_ref[t, rows, cols]
                acc = jnp.zeros(idx.shape, F32)
                for b in range(NUM_BUCKETS):
                    acc = jnp.where(idx == b, tab_ref[b, h], acc)
                o_ref[t, 0, rows, cols] = jnp.where(idx < 0, NEG_BIG, acc)


def _bias_tiles(tab):
    T, TK = Q_TILE, ATT_TILE
    n_tiles = T // TK + 1
    c = np.arange(TK)[:, None]
    r = np.arange(T)[None, :]
    dist = [r - c + TK * (1 - m) for m in range(n_tiles)]
    idx = jnp.asarray(np.stack([np.where(d >= 0, _rel_bucket_np(np.maximum(d, 0)), -1)
                                for d in dist]).astype(np.int32))

    def kind(d):
        return "masked" if (d < 0).all() else "zero" if (d >= MAX_DISTANCE).all() else "band"

    kinds = tuple(tuple(tuple(kind(d[i:i + LANES, j:j + LANES]) for j in range(0, T, LANES))
                        for i in range(0, TK, LANES)) for d in dist)
    return pl.pallas_call(
        functools.partial(_bias_body, kinds),
        grid=(N_HEADS,),
        in_specs=[pl.BlockSpec(memory_space=pltpu.SMEM),
                  pl.BlockSpec((n_tiles, TK, T), lambda h: (0, 0, 0))],
        out_specs=pl.BlockSpec((n_tiles, 1, TK, T), lambda h: (0, h, 0, 0)),
        out_shape=jax.ShapeDtypeStruct((n_tiles, N_HEADS, TK, T), F32),
        name="bias_tiles",
    )(tab, idx)


def kernel(x, rel_bias, ffn1_norm, ffn1_w_gate, ffn1_w_up, ffn1_w_down, mix_norm, w_in, conv_w, conv_b,
           conv_norm, q_norm, k_norm, lambda_q1, lambda_k1, lambda_q2, lambda_k2, subln_norm, w_out,
           ffn2_norm, ffn2_w_gate, ffn2_w_up, ffn2_w_down):
    B, S, _ = x.shape
    depth = w_in.shape[0]
    assert S % Q_TILE == 0 and Q_TILE == 2 * ATT_TILE and ROW_TILE == ATT_TILE and ATT_TILE >= MAX_DISTANCE

    grp = np.arange(GROUP_MM) // HEAD_DIM
    gsum = jnp.asarray(grp[:, None] == grp[None, :], BF16)
    tab = (rel_bias - rel_bias[NUM_BUCKETS - 1:]).astype(F32) * LOG2E
    bias = _bias_tiles(tab)
    row2 = lambda a: a.reshape(1, -1).astype(F32)

    xf = x.reshape(B * S, D_MODEL)
    for l in range(depth):
        lam_init = 0.8 - 0.6 * math.exp(-0.3 * l)
        qg = q_norm[l].astype(F32) * (LOG2E / math.sqrt(HEAD_DIM))
        kg = k_norm[l].astype(F32)
        qkg = jnp.concatenate([jnp.tile(qg, 2 * N_HEADS), jnp.tile(kg, 2 * N_HEADS)]).reshape(1, -1)
        score_bound = (HEAD_DIM * 1.02) * jnp.max(jnp.abs(qg)) * jnp.max(jnp.abs(kg)) + jnp.max(jnp.abs(tab))
        cw8 = jnp.broadcast_to(conv_w[l].astype(F32)[:, None, :], (CONV_K, SUBLANES, CONV_CH))
        x1, c, q, k, vt = _ffn_proj(xf, row2(ffn1_norm[l]), ffn1_w_gate[l].astype(BF16),
                                   ffn1_w_up[l].astype(BF16), ffn1_w_down[l].astype(BF16),
                                   row2(mix_norm[l]), w_in[l].astype(BF16), gsum, qkg, cw8,
                                   row2(conv_b[l]), row2(conv_norm[l]), S)
        lamv = jnp.stack([lambda_q1[l], lambda_k1[l], lambda_q2[l], lambda_k2[l]]).astype(F32)
        sg = row2(subln_norm[l]) * (1.0 - lam_init)
        att = lax.cond(score_bound <= SAFE_EXP2_BOUND,
                       functools.partial(_diff_attn, False, lam_init),
                       functools.partial(_diff_attn, True, lam_init),
                       q.reshape(B, S, DIFF_WIDTH), k.reshape(B, S, DIFF_WIDTH), vt, bias, lamv, sg)
        xf = _mix_ffn(x1, c, att.reshape(B * S, DIFF_WIDTH), w_out[l, :CONV_CH].astype(BF16),
                      w_out[l, CONV_CH:].astype(BF16), row2(ffn2_norm[l]), ffn2_w_gate[l].astype(BF16),
                      ffn2_w_up[l].astype(BF16), ffn2_w_down[l].astype(BF16))
    return xf.reshape(B, S, D_MODEL)
```

```python
import functools
import math

import jax
import jax.numpy as jnp
import numpy as np
from jax import lax
from jax.experimental import pallas as pl
from jax.experimental.pallas import tpu as pltpu

F32 = jnp.float32
BF16 = jnp.bfloat16

D_MODEL = 1024
CONV_CH = 512
CONV_K = 31
DIFF_WIDTH = 512
HEAD_DIM = 64
N_HEADS = 4
D_FF = 2816
NUM_BUCKETS = 32
MAX_DISTANCE = 128
EPS = 1e-6
LOG2E = 1.4426950408889634
NEG_BIG = -1e30
SAFE_EXP2_BOUND = 90.0

FF_CHUNK = 256
N_FF_CHUNKS = D_FF // FF_CHUNK
ROW_TILE = 512
MIX_TILE = 1024
ATT_TILE = 512
Q_TILE = 1024
FAR_STEP = 4
SIDE_WORK_LAG = 3
LANES = 128
SUBLANES = 8
CONV_HALO = 32
CONV_ROWS = 16
GROUP_MM = 256
VMEM_LIMIT = 56 * 1024 * 1024


def _rms(x, g):
    ms = jnp.mean(x * x, axis=-1, keepdims=True)
    return x * lax.rsqrt(ms + EPS) * g


def _zero_after(v):
    bits = pltpu.bitcast(v[0:1, :], jnp.uint32)
    return pltpu.bitcast((bits >> 16) >> 16, F32)


def _ffn(x, g_ref, wg_ref, wu_ref, wd_ref, act_ref, side_work=None):
    h = _rms(x, g_ref[...]).astype(BF16)
    zs = [None] * SIDE_WORK_LAG
    for c in range(N_FF_CHUNKS):
        lo = c * FF_CHUNK
        gate = jnp.dot(h, wg_ref[:, lo:lo + FF_CHUNK], preferred_element_type=F32)
        up = jnp.dot(h, wu_ref[:, lo:lo + FF_CHUNK], preferred_element_type=F32)
        z = zs.pop(0)
        if z is not None:
            up = up + z
        act_ref[:, lo:lo + FF_CHUNK] = (gate * jax.nn.sigmoid(gate) * up).astype(BF16)
        zs.append(side_work(c, gate) if side_work is not None else None)
    y = jnp.dot(act_ref[...], wd_ref[...], preferred_element_type=F32)
    return x + 0.5 * y


def _conv_prev_tile(ubuf, ush, cw_ref, cb_ref, cn_ref, c_ref):
    tm = c_ref.shape[0]
    base = CONV_HALO - (CONV_K - 1)
    n_pieces = N_FF_CHUNKS
    rows_per = -(-tm // (n_pieces * SUBLANES)) * SUBLANES

    def rows_after(r0, nr, z):
        groups = nr // SUBLANES
        bias = cb_ref[...] + z
        acc = jnp.broadcast_to(bias, (groups, SUBLANES, CONV_CH))
        for j in range(CONV_K):
            a, b = divmod(base + j, SUBLANES)
            lo = r0 + a * SUBLANES
            rows = ubuf[lo:lo + nr, :] if b == 0 else ush[b - 1, lo:lo + nr, :]
            acc = acc + cw_ref[j][None] * rows.reshape(groups, SUBLANES, CONV_CH)
        c = _rms(acc.reshape(nr, CONV_CH), cn_ref[...])
        c = c * jax.nn.sigmoid(c)
        c_ref[r0:r0 + nr, :] = c.astype(c_ref.dtype)
        return _zero_after(jnp.sum(c, axis=0, keepdims=True))

    def piece(p, gate):
        zg = _zero_after(gate)
        z = jnp.concatenate([zg] * (CONV_CH // FF_CHUNK), axis=1)
        r0 = p * rows_per
        end = min(r0 + rows_per, tm)
        while r0 < end:
            nr = min(CONV_ROWS, end - r0)
            z = rows_after(r0, nr, z)
            r0 += nr
        return z[:, :FF_CHUNK]

    return piece


def _ffn_proj_body(seq_tiles, x_ref, n1_ref, wg_ref, wu_ref, wd_ref, nm_ref, win_ref, gsum_ref, qkg_ref,
                   cw_ref, cb_ref, cn_ref, xo_ref, c_ref, q_ref, k_ref, vt_ref, act_ref, ubuf, ush):
    i = pl.program_id(0)
    tm = x_ref.shape[0]

    @pl.when(i == 0)
    def _():
        ubuf[...] = jnp.zeros(ubuf.shape, F32)
        ush[...] = jnp.zeros(ush.shape, F32)

    conv_piece = _conv_prev_tile(ubuf, ush, cw_ref, cb_ref, cn_ref, c_ref)
    x1 = _ffn(x_ref[...], n1_ref, wg_ref, wu_ref, wd_ref, act_ref, side_work=conv_piece)
    xo_ref[...] = x1
    h = _rms(x1, nm_ref[...]).astype(BF16)
    ag = jnp.dot(h, win_ref[:, 0:2 * CONV_CH], preferred_element_type=F32)
    u = ag[:, :CONV_CH] * jax.nn.sigmoid(ag[:, CONV_CH:])
    qk = jnp.dot(h, win_ref[:, 2 * CONV_CH:2 * CONV_CH + 2 * DIFF_WIDTH], preferred_element_type=F32)
    sq = (qk * qk).astype(BF16)
    ss = jnp.concatenate(
        [jnp.dot(sq[:, c * GROUP_MM:(c + 1) * GROUP_MM], gsum_ref[...], preferred_element_type=F32)
         for c in range(2 * DIFF_WIDTH // GROUP_MM)], axis=1)
    qkn = qk * lax.rsqrt(ss * (1.0 / HEAD_DIM) + EPS) * qkg_ref[...]
    q_ref[...] = qkn[:, :DIFF_WIDTH].astype(BF16)
    k_ref[...] = qkn[:, DIFF_WIDTH:].astype(BF16)
    v = jnp.dot(h, win_ref[:, 2 * CONV_CH + 2 * DIFF_WIDTH:], preferred_element_type=F32)
    vt_ref[0, :, 0] = v.T.reshape(N_HEADS, 2 * HEAD_DIM, tm).astype(BF16)

    tail = ubuf[tm:tm + CONV_HALO, :]
    ubuf[0:CONV_HALO, :] = jnp.where(i % seq_tiles == 0, 0.0, tail)
    ubuf[CONV_HALO:, :] = u
    for b in range(1, SUBLANES):
        ush[b - 1] = ubuf[b:b + ush.shape[1], :]


def _const_spec(shape):
    nd = len(shape)
    return pl.BlockSpec(shape, lambda i: (0,) * nd, pipeline_mode=pl.Buffered(1))


def _ffn_proj(x, n1, wg, wu, wd, nm, win, gsum, qkg, cw, cb, cn, seq):
    t = x.shape[0]
    tm = ROW_TILE
    n = t // tm
    cur = lambda w: pl.BlockSpec((tm, w), lambda i: (jnp.minimum(i, n - 1), 0))
    prev = lambda w: pl.BlockSpec((tm, w), lambda i: (jnp.maximum(i - 1, 0), 0))
    consts = (n1, wg, wu, wd, nm, win, gsum, qkg, cw, cb, cn)
    seq_tiles = seq // tm

    def vt_index(i):
        ii = jnp.minimum(i, n - 1)
        return (ii // seq_tiles, 0, ii % seq_tiles, 0, 0)

    return pl.pallas_call(
        functools.partial(_ffn_proj_body, seq_tiles),
        grid=(n + 1,),
        in_specs=[cur(D_MODEL)] + [_const_spec(a.shape) for a in consts],
        out_specs=[cur(D_MODEL), prev(CONV_CH), cur(DIFF_WIDTH), cur(DIFF_WIDTH),
                   pl.BlockSpec((1, N_HEADS, 1, 2 * HEAD_DIM, tm), vt_index)],
        out_shape=[jax.ShapeDtypeStruct((t, D_MODEL), F32), jax.ShapeDtypeStruct((t, CONV_CH), BF16),
                   jax.ShapeDtypeStruct((t, DIFF_WIDTH), BF16), jax.ShapeDtypeStruct((t, DIFF_WIDTH), BF16),
                   jax.ShapeDtypeStruct((t // seq, N_HEADS, seq_tiles, 2 * HEAD_DIM, tm), BF16)],
        scratch_shapes=[pltpu.VMEM((tm, D_FF), BF16), pltpu.VMEM((tm + CONV_HALO, CONV_CH), F32),
                        pltpu.VMEM((SUBLANES - 1, tm + CONV_HALO - SUBLANES, CONV_CH), F32)],
        compiler_params=pltpu.CompilerParams(dimension_semantics=("arbitrary",),
                                             vmem_limit_bytes=VMEM_LIMIT),
        name="ffn_proj",
    )(x, *consts)


def _mix_ffn_body(x_ref, c_ref, att_ref, woc_ref, wod_ref, n2_ref, wg_ref, wu_ref, wd_ref, o_ref,
                  act_ref):
    mix = (jnp.dot(c_ref[...], woc_ref[...], preferred_element_type=F32)
           + jnp.dot(att_ref[...], wod_ref[...], preferred_element_type=F32))
    x1 = x_ref[...] + mix
    o_ref[...] = _ffn(x1, n2_ref, wg_ref, wu_ref, wd_ref, act_ref)


def _mix_ffn(x, c, att, woc, wod, n2, wg, wu, wd):
    t = x.shape[0]
    tm = MIX_TILE
    row = lambda w: pl.BlockSpec((tm, w), lambda i: (i, 0))
    consts = (woc, wod, n2, wg, wu, wd)
    return pl.pallas_call(
        _mix_ffn_body,
        grid=(t // tm,),
        in_specs=[row(D_MODEL), row(CONV_CH), row(DIFF_WIDTH)] + [_const_spec(a.shape) for a in consts],
        out_specs=row(D_MODEL),
        out_shape=jax.ShapeDtypeStruct((t, D_MODEL), F32),
        scratch_shapes=[pltpu.VMEM((tm, D_FF), BF16)],
        compiler_params=pltpu.CompilerParams(dimension_semantics=("arbitrary",),
                                             vmem_limit_bytes=VMEM_LIMIT),
        name="mix_ffn",
    )(x, c, att, *consts)


def _attn_body(lam_init, online, lamv_ref, q_ref, k_ref, vt_ref, bias_ref, sg_ref, o_ref,
               qbd_ref, m_ref, l_ref, acc1_ref, acc2_ref):
    T = Q_TILE
    TK = ATT_TILE
    ratio = T // TK
    i = pl.program_id(2)

    qt = q_ref[0].astype(F32).T
    row = lax.broadcasted_iota(jnp.int32, qt.shape, 0)
    qbd_ref[:, :T] = jnp.where(row < HEAD_DIM, qt, 0.0).astype(BF16)
    qbd_ref[:, T:] = jnp.where(row >= HEAD_DIM, qt, 0.0).astype(BF16)
    if online:
        m_ref[...] = jnp.full(m_ref.shape, NEG_BIG, F32)
        l_ref[...] = jnp.zeros(l_ref.shape, F32)
        acc1_ref[...] = jnp.zeros(acc1_ref.shape, F32)
        acc2_ref[...] = jnp.zeros(acc2_ref.shape, F32)

    H2 = T // 2

    def on_upper_half(v, fill):
        pad = jnp.full((1, H2), fill, F32)
        return jnp.concatenate([pad, v[:, :H2], pad, v[:, H2:]], axis=1)

    def block(j, nblk, first_bias=None, plain=0, half_last=False, first=False):
        n_main = nblk - 1 if half_last else nblk
        kb = k_ref[0, pl.ds(pl.multiple_of(j * TK, TK), nblk * TK), :]
        s = jnp.dot(kb[:n_main * TK], qbd_ref[...], preferred_element_type=F32)
        if first_bias is not None:
            nb = n_main - plain
            b = bias_ref[first_bias:first_bias + nb, 0].reshape(nb * TK, T)
            biased = s[plain * TK:] + jnp.concatenate([b, b], axis=1)
            s = biased if plain == 0 else jnp.concatenate([s[:plain * TK], biased], axis=0)
        vtb = jnp.concatenate([vt_ref[0, 0, j + t] for t in range(n_main)], axis=1)
        if half_last:
            q_half = jnp.concatenate([qbd_ref[:, H2:T], qbd_ref[:, T + H2:]], axis=1)
            b_last = bias_ref[first_bias + n_main - plain, 0][:, H2:]
            s_last = (jnp.dot(kb[n_main * TK:], q_half, preferred_element_type=F32)
                      + jnp.concatenate([b_last, b_last], axis=1))
            vt_last = vt_ref[0, 0, j + n_main]
        if not online:
            p = jnp.exp2(s)
            lsum = jnp.sum(p, axis=0, keepdims=True)
            pb = p.astype(BF16)
            pv1 = jnp.dot(vtb, pb[:, :T], preferred_element_type=F32)
            pv2 = jnp.dot(vtb, pb[:, T:], preferred_element_type=F32)
            if half_last:
                p_last = jnp.exp2(s_last)
                lsum = lsum + on_upper_half(jnp.sum(p_last, axis=0, keepdims=True), 0.0)
                pb_last = p_last.astype(BF16)
            acc1_ref[...] = pv1 if first else acc1_ref[...] + pv1
            acc2_ref[...] = pv2 if first else acc2_ref[...] + pv2
            l_ref[...] = lsum if first else l_ref[...] + lsum
            if half_last:
                acc1_ref[:, H2:] += jnp.dot(vt_last, pb_last[:, :H2], preferred_element_type=F32)
                acc2_ref[:, H2:] += jnp.dot(vt_last, pb_last[:, H2:], preferred_element_type=F32)
            return
        m_old = m_ref[...]
        cmax = jnp.max(s, axis=0, keepdims=True)
        if half_last:
            cmax = jnp.maximum(cmax, on_upper_half(jnp.max(s_last, axis=0, keepdims=True), NEG_BIG))
        m_new = jnp.maximum(m_old, cmax)
        alpha = jnp.exp2(m_old - m_new)
        p = jnp.exp2(s - m_new)
        lsum = jnp.sum(p, axis=0, keepdims=True)
        pb = p.astype(BF16)
        acc1_ref[...] = alpha[:, :T] * acc1_ref[...] + jnp.dot(vtb, pb[:, :T], preferred_element_type=F32)
        acc2_ref[...] = alpha[:, T:] * acc2_ref[...] + jnp.dot(vtb, pb[:, T:], preferred_element_type=F32)
        if half_last:
            m_half = jnp.concatenate([m_new[:, H2:T], m_new[:, T + H2:]], axis=1)
            p_last = jnp.exp2(s_last - m_half)
            lsum = lsum + on_upper_half(jnp.sum(p_last, axis=0, keepdims=True), 0.0)
            pb_last = p_last.astype(BF16)
            acc1_ref[:, H2:] += jnp.dot(vt_last, pb_last[:, :H2], preferred_element_type=F32)
            acc2_ref[:, H2:] += jnp.dot(vt_last, pb_last[:, H2:], preferred_element_type=F32)
        l_ref[...] = alpha * l_ref[...] + lsum
        m_ref[...] = m_new

    @pl.when(i > 0)
    def _():
        block(ratio * i - 2, ratio + 2, 0, plain=1, half_last=True, first=True)

    @pl.when(i == 0)
    def _():
        block(0, ratio, 1, half_last=True, first=True)

    n_far = jnp.maximum(ratio * i - 2, 0)

    def far_step(jj, carry):
        block(FAR_STEP * jj, FAR_STEP)
        return carry

    lax.fori_loop(0, n_far // FAR_STEP, far_step, 0)
    done = (n_far // FAR_STEP) * FAR_STEP
    size = FAR_STEP // 2
    while size >= 2:
        take = (n_far - done) >= size

        @pl.when(take)
        def _(done=done, size=size):
            block(done, size)

        done = done + jnp.where(take, size, 0)
        size //= 2

    lv = lamv_ref[...]
    lam = (jnp.exp(jnp.sum(lv[0:1] * lv[1:2], axis=1, keepdims=True))
           - jnp.exp(jnp.sum(lv[2:3] * lv[3:4], axis=1, keepdims=True)) + lam_init)
    l = l_ref[...]
    d = acc1_ref[...] / l[:, :T] - lam * (acc2_ref[...] / l[:, T:])
    ms = jnp.mean(d * d, axis=0, keepdims=True)
    y = d * lax.rsqrt(ms + EPS)
    o_ref[0] = (y.T * sg_ref[...]).astype(o_ref.dtype)


def _diff_attn(online, lam_init, q, k, vt, bias, lamv, sg):
    b, s, _ = q.shape
    T = Q_TILE
    TK = ATT_TILE
    nq = s // T
    return pl.pallas_call(
        functools.partial(_attn_body, lam_init, online),
        grid=(b, N_HEADS, nq),
        in_specs=[
            pl.BlockSpec(lamv.shape, lambda bi, h, i: (0, 0)),
            pl.BlockSpec((1, T, 2 * HEAD_DIM), lambda bi, h, i: (bi, i, h)),
            pl.BlockSpec((1, s, 2 * HEAD_DIM), lambda bi, h, i: (bi, 0, h)),
            pl.BlockSpec((1, 1, s // TK, 2 * HEAD_DIM, TK), lambda bi, h, i: (bi, h, 0, 0, 0)),
            pl.BlockSpec((bias.shape[0], 1, TK, T), lambda bi, h, i: (0, h, 0, 0)),
            pl.BlockSpec(sg.shape, lambda bi, h, i: (0, 0)),
        ],
        out_specs=pl.BlockSpec((1, T, 2 * HEAD_DIM), lambda bi, h, i: (bi, i, h)),
        out_shape=jax.ShapeDtypeStruct((b, s, DIFF_WIDTH), BF16),
        scratch_shapes=[pltpu.VMEM((2 * HEAD_DIM, 2 * T), BF16), pltpu.VMEM((1, 2 * T), F32),
                        pltpu.VMEM((1, 2 * T), F32), pltpu.VMEM((2 * HEAD_DIM, T), F32),
                        pltpu.VMEM((2 * HEAD_DIM, T), F32)],
        compiler_params=pltpu.CompilerParams(dimension_semantics=("arbitrary", "arbitrary", "arbitrary"),
                                             vmem_limit_bytes=VMEM_LIMIT),
        name="diff_attn_online" if online else "diff_attn",
    )(lamv, q, k, vt, bias, sg)


def _rel_bucket_np(n):
    max_exact = NUM_BUCKETS // 2
    nf = np.maximum(n, 1).astype(np.float32)
    large = max_exact + (np.log(nf / max_exact) / math.log(MAX_DISTANCE / max_exact)
                         * (NUM_BUCKETS - max_exact)).astype(np.int32)
    large = np.minimum(large, NUM_BUCKETS - 1)
    return np.where(n < max_exact, n, large)


def _bias_body(kinds, tab_ref, idx_ref, o_ref):
    h = pl.program_id(0)
    for t, tile_kinds in enumerate(kinds):
        for i, row_kinds in enumerate(tile_kinds):
            for j, kind in enumerate(row_kinds):
                rows, cols = slice(i * LANES, (i + 1) * LANES), slice(j * LANES, (j + 1) * LANES)
                if kind != "band":
                    fill = NEG_BIG if kind == "masked" else 0.0
                    o_ref[t, 0, rows, cols] = jnp.full((LANES, LANES), fill, F32)
                    continue
                idx = idx_ref[t, rows, cols]
                acc = jnp.zeros(idx.shape, F32)
                for b in range(NUM_BUCKETS):
                    acc = jnp.where(idx == b, tab_ref[b, h], acc)
                o_ref[t, 0, rows, cols] = jnp.where(idx < 0, NEG_BIG, acc)


def _bias_tiles(tab):
    T, TK = Q_TILE, ATT_TILE
    n_tiles = T // TK + 1
    c = np.arange(TK)[:, None]
    r = np.arange(T)[None, :]
    dist = [r - c + TK * (1 - m) for m in range(n_tiles)]
    idx = jnp.asarray(np.stack([np.where(d >= 0, _rel_bucket_np(np.maximum(d, 0)), -1)
                                for d in dist]).astype(np.int32))

    def kind(d):
        return "masked" if (d < 0).all() else "zero" if (d >= MAX_DISTANCE).all() else "band"

    kinds = tuple(tuple(tuple(kind(d[i:i + LANES, j:j + LANES]) for j in range(0, T, LANES))
                        for i in range(0, TK, LANES)) for d in dist)
    return pl.pallas_call(
        functools.partial(_bias_body, kinds),
        grid=(N_HEADS,),
        in_specs=[pl.BlockSpec(memory_space=pltpu.SMEM),
                  pl.BlockSpec((n_tiles, TK, T), lambda h: (0, 0, 0))],
        out_specs=pl.BlockSpec((n_tiles, 1, TK, T), lambda h: (0, h, 0, 0)),
        out_shape=jax.ShapeDtypeStruct((n_tiles, N_HEADS, TK, T), F32),
        name="bias_tiles",
    )(tab, idx)


def kernel(x, rel_bias, ffn1_norm, ffn1_w_gate, ffn1_w_up, ffn1_w_down, mix_norm, w_in, conv_w, conv_b,
           conv_norm, q_norm, k_norm, lambda_q1, lambda_k1, lambda_q2, lambda_k2, subln_norm, w_out,
           ffn2_norm, ffn2_w_gate, ffn2_w_up, ffn2_w_down):
    B, S, _ = x.shape
    depth = w_in.shape[0]
    assert S % Q_TILE == 0 and Q_TILE == 2 * ATT_TILE and ROW_TILE == ATT_TILE and ATT_TILE >= MAX_DISTANCE

    grp = np.arange(GROUP_MM) // HEAD_DIM
    gsum = jnp.asarray(grp[:, None] == grp[None, :], BF16)
    tab = (rel_bias - rel_bias[NUM_BUCKETS - 1:]).astype(F32) * LOG2E
    bias = _bias_tiles(tab)
    row2 = lambda a: a.reshape(1, -1).astype(F32)

    xf = x.reshape(B * S, D_MODEL)
    for l in range(depth):
        lam_init = 0.8 - 0.6 * math.exp(-0.3 * l)
        qg = q_norm[l].astype(F32) * (LOG2E / math.sqrt(HEAD_DIM))
        kg = k_norm[l].astype(F32)
        qkg = jnp.concatenate([jnp.tile(qg, 2 * N_HEADS), jnp.tile(kg, 2 * N_HEADS)]).reshape(1, -1)
        score_bound = (HEAD_DIM * 1.02) * jnp.max(jnp.abs(qg)) * jnp.max(jnp.abs(kg)) + jnp.max(jnp.abs(tab))
        cw8 = jnp.broadcast_to(conv_w[l].astype(F32)[:, None, :], (CONV_K, SUBLANES, CONV_CH))
        x1, c, q, k, vt = _ffn_proj(xf, row2(ffn1_norm[l]), ffn1_w_gate[l].astype(BF16),
                                   ffn1_w_up[l].astype(BF16), ffn1_w_down[l].astype(BF16),
                                   row2(mix_norm[l]), w_in[l].astype(BF16), gsum, qkg, cw8,
                                   row2(conv_b[l]), row2(conv_norm[l]), S)
        lamv = jnp.stack([lambda_q1[l], lambda_k1[l], lambda_q2[l], lambda_k2[l]]).astype(F32)
        sg = row2(subln_norm[l]) * (1.0 - lam_init)
        att = lax.cond(score_bound <= SAFE_EXP2_BOUND,
                       functools.partial(_diff_attn, False, lam_init),
                       functools.partial(_diff_attn, True, lam_init),
                       q.reshape(B, S, DIFF_WIDTH), k.reshape(B, S, DIFF_WIDTH), vt, bias, lamv, sg)
        xf = _mix_ffn(x1, c, att.reshape(B * S, DIFF_WIDTH), w_out[l, :CONV_CH].astype(BF16),
                      w_out[l, CONV_CH:].astype(BF16), row2(ffn2_norm[l]), ffn2_w_gate[l].astype(BF16),
                      ffn2_w_up[l].astype(BF16), ffn2_w_down[l].astype(BF16))
    return xf.reshape(B, S, D_MODEL)
```

```python
import functools
import math

import jax
import jax.numpy as jnp
import numpy as np
from jax import lax
from jax.experimental import pallas as pl
from jax.experimental.pallas import tpu as pltpu

F32 = jnp.float32
BF16 = jnp.bfloat16

D_MODEL = 1024
CONV_CH = 512
CONV_K = 31
DIFF_WIDTH = 512
HEAD_DIM = 64
N_HEADS = 4
D_FF = 2816
NUM_BUCKETS = 32
MAX_DISTANCE = 128
EPS = 1e-6
LOG2E = 1.4426950408889634
NEG_BIG = -1e30
SAFE_EXP2_BOUND = 90.0

FF_CHUNK = 256
N_FF_CHUNKS = D_FF // FF_CHUNK
ROW_TILE = 512
MIX_TILE = 1024
ATT_TILE = 512
Q_TILE = 1024
FAR_STEP = 4
SIDE_WORK_LAG = 2
LANES = 128
SUBLANES = 8
CONV_HALO = 32
CONV_ROWS = 16
GROUP_MM = 256
VMEM_LIMIT = 56 * 1024 * 1024


def _rms(x, g):
    ms = jnp.mean(x * x, axis=-1, keepdims=True)
    return x * lax.rsqrt(ms + EPS) * g


def _zero_after(v):
    bits = pltpu.bitcast(v[0:1, :], jnp.uint32)
    return pltpu.bitcast((bits >> 16) >> 16, F32)


def _ffn(x, g_ref, wg_ref, wu_ref, wd_ref, act_ref, side_work=None):
    h = _rms(x, g_ref[...]).astype(BF16)
    zs = [None] * SIDE_WORK_LAG
    for c in range(N_FF_CHUNKS):
        lo = c * FF_CHUNK
        gate = jnp.dot(h, wg_ref[:, lo:lo + FF_CHUNK], preferred_element_type=F32)
        up = jnp.dot(h, wu_ref[:, lo:lo + FF_CHUNK], preferred_element_type=F32)
        z = zs.pop(0)
        if z is not None:
            up = up + z
        act_ref[:, lo:lo + FF_CHUNK] = (gate * jax.nn.sigmoid(gate) * up).astype(BF16)
        zs.append(side_work(c, gate) if side_work is not None else None)
    y = jnp.dot(act_ref[...], wd_ref[...], preferred_element_type=F32)
    return x + 0.5 * y


def _conv_prev_tile(ubuf, ush, cw_ref, cb_ref, cn_ref, c_ref):
    tm = c_ref.shape[0]
    base = CONV_HALO - (CONV_K - 1)
    n_pieces = N_FF_CHUNKS
    rows_per = -(-tm // (n_pieces * SUBLANES)) * SUBLANES

    def rows_after(r0, nr, z):
        groups = nr // SUBLANES
        bias = cb_ref[...] + z
        acc = jnp.broadcast_to(bias, (groups, SUBLANES, CONV_CH))
        for j in range(CONV_K):
            a, b = divmod(base + j, SUBLANES)
            lo = r0 + a * SUBLANES
            rows = ubuf[lo:lo + nr, :] if b == 0 else ush[b - 1, lo:lo + nr, :]
            acc = acc + cw_ref[j][None] * rows.reshape(groups, SUBLANES, CONV_CH)
        c = _rms(acc.reshape(nr, CONV_CH), cn_ref[...])
        c = c * jax.nn.sigmoid(c)
        c_ref[r0:r0 + nr, :] = c.astype(c_ref.dtype)
        return _zero_after(jnp.sum(c, axis=0, keepdims=True))

    def piece(p, gate):
        zg = _zero_after(gate)
        z = jnp.concatenate([zg] * (CONV_CH // FF_CHUNK), axis=1)
        r0 = p * rows_per
        end = min(r0 + rows_per, tm)
        while r0 < end:
            nr = min(CONV_ROWS, end - r0)
            z = rows_after(r0, nr, z)
            r0 += nr
        return z[:, :FF_CHUNK]

    return piece


def _ffn_proj_body(seq_tiles, x_ref, n1_ref, wg_ref, wu_ref, wd_ref, nm_ref, win_ref, gsum_ref, qkg_ref,
                   cw_ref, cb_ref, cn_ref, xo_ref, c_ref, q_ref, k_ref, vt_ref, act_ref, ubuf, ush):
    i = pl.program_id(0)
    tm = x_ref.shape[0]

    @pl.when(i == 0)
    def _():
        ubuf[...] = jnp.zeros(ubuf.shape, F32)
        ush[...] = jnp.zeros(ush.shape, F32)

    conv_piece = _conv_prev_tile(ubuf, ush, cw_ref, cb_ref, cn_ref, c_ref)
    last = pl.num_programs(0) - 1

    @pl.when(i == last)
    def _():
        idle = jnp.zeros((SUBLANES, FF_CHUNK), F32)
        for p in range(N_FF_CHUNKS):
            conv_piece(p, idle)

    @pl.when(i < last)
    def _():
        x1 = _ffn(x_ref[...], n1_ref, wg_ref, wu_ref, wd_ref, act_ref, side_work=conv_piece)
        xo_ref[...] = x1
        h = _rms(x1, nm_ref[...]).astype(BF16)
        ag = jnp.dot(h, win_ref[:, 0:2 * CONV_CH], preferred_element_type=F32)
        u = ag[:, :CONV_CH] * jax.nn.sigmoid(ag[:, CONV_CH:])
        qk = jnp.dot(h, win_ref[:, 2 * CONV_CH:2 * CONV_CH + 2 * DIFF_WIDTH], preferred_element_type=F32)
        sq = (qk * qk).astype(BF16)
        ss = jnp.concatenate(
            [jnp.dot(sq[:, c * GROUP_MM:(c + 1) * GROUP_MM], gsum_ref[...], preferred_element_type=F32)
             for c in range(2 * DIFF_WIDTH // GROUP_MM)], axis=1)
        qkn = qk * lax.rsqrt(ss * (1.0 / HEAD_DIM) + EPS) * qkg_ref[...]
        q_ref[...] = qkn[:, :DIFF_WIDTH].astype(BF16)
        k_ref[...] = qkn[:, DIFF_WIDTH:].astype(BF16)
        v = jnp.dot(h, win_ref[:, 2 * CONV_CH + 2 * DIFF_WIDTH:], preferred_element_type=F32)
        vt_ref[0, :, 0] = v.T.reshape(N_HEADS, 2 * HEAD_DIM, tm).astype(BF16)

        tail = ubuf[tm:tm + CONV_HALO, :]
        ubuf[0:CONV_HALO, :] = jnp.where(i % seq_tiles == 0, 0.0, tail)
        ubuf[CONV_HALO:, :] = u
        for b in range(1, SUBLANES):
            ush[b - 1] = ubuf[b:b + ush.shape[1], :]


def _const_spec(shape):
    nd = len(shape)
    return pl.BlockSpec(shape, lambda i: (0,) * nd, pipeline_mode=pl.Buffered(1))


def _ffn_proj(x, n1, wg, wu, wd, nm, win, gsum, qkg, cw, cb, cn, seq):
    t = x.shape[0]
    tm = ROW_TILE
    n = t // tm
    cur = lambda w: pl.BlockSpec((tm, w), lambda i: (jnp.minimum(i, n - 1), 0))
    prev = lambda w: pl.BlockSpec((tm, w), lambda i: (jnp.maximum(i - 1, 0), 0))
    consts = (n1, wg, wu, wd, nm, win, gsum, qkg, cw, cb, cn)
    seq_tiles = seq // tm

    def vt_index(i):
        ii = jnp.minimum(i, n - 1)
        return (ii // seq_tiles, 0, ii % seq_tiles, 0, 0)

    return pl.pallas_call(
        functools.partial(_ffn_proj_body, seq_tiles),
        grid=(n + 1,),
        in_specs=[cur(D_MODEL)] + [_const_spec(a.shape) for a in consts],
        out_specs=[cur(D_MODEL), prev(CONV_CH), cur(DIFF_WIDTH), cur(DIFF_WIDTH),
                   pl.BlockSpec((1, N_HEADS, 1, 2 * HEAD_DIM, tm), vt_index)],
        out_shape=[jax.ShapeDtypeStruct((t, D_MODEL), F32), jax.ShapeDtypeStruct((t, CONV_CH), BF16),
                   jax.ShapeDtypeStruct((t, DIFF_WIDTH), BF16), jax.ShapeDtypeStruct((t, DIFF_WIDTH), BF16),
                   jax.ShapeDtypeStruct((t // seq, N_HEADS, seq_tiles, 2 * HEAD_DIM, tm), BF16)],
        scratch_shapes=[pltpu.VMEM((tm, D_FF), BF16), pltpu.VMEM((tm + CONV_HALO, CONV_CH), F32),
                        pltpu.VMEM((SUBLANES - 1, tm + CONV_HALO - SUBLANES, CONV_CH), F32)],
        compiler_params=pltpu.CompilerParams(dimension_semantics=("arbitrary",),
                                             vmem_limit_bytes=VMEM_LIMIT),
        name="ffn_proj",
    )(x, *consts)


def _mix_ffn_body(x_ref, c_ref, att_ref, woc_ref, wod_ref, n2_ref, wg_ref, wu_ref, wd_ref, o_ref,
                  act_ref):
    mix = (jnp.dot(c_ref[...], woc_ref[...], preferred_element_type=F32)
           + jnp.dot(att_ref[...], wod_ref[...], preferred_element_type=F32))
    x1 = x_ref[...] + mix
    o_ref[...] = _ffn(x1, n2_ref, wg_ref, wu_ref, wd_ref, act_ref)


def _mix_ffn(x, c, att, woc, wod, n2, wg, wu, wd):
    t = x.shape[0]
    tm = MIX_TILE
    row = lambda w: pl.BlockSpec((tm, w), lambda i: (i, 0))
    consts = (woc, wod, n2, wg, wu, wd)
    return pl.pallas_call(
        _mix_ffn_body,
        grid=(t // tm,),
        in_specs=[row(D_MODEL), row(CONV_CH), row(DIFF_WIDTH)] + [_const_spec(a.shape) for a in consts],
        out_specs=row(D_MODEL),
        out_shape=jax.ShapeDtypeStruct((t, D_MODEL), F32),
        scratch_shapes=[pltpu.VMEM((tm, D_FF), BF16)],
        compiler_params=pltpu.CompilerParams(dimension_semantics=("arbitrary",),
                                             vmem_limit_bytes=VMEM_LIMIT),
        name="mix_ffn",
    )(x, c, att, *consts)


def _attn_body(lam_init, online, lamv_ref, q_ref, k_ref, vt_ref, bias_ref, sg_ref, o_ref,
               qbd_ref, m_ref, l_ref, acc1_ref, acc2_ref):
    T = Q_TILE
    TK = ATT_TILE
    ratio = T // TK
    i = pl.program_id(2)

    qt = q_ref[0].astype(F32).T
    row = lax.broadcasted_iota(jnp.int32, qt.shape, 0)
    qbd_ref[:, :T] = jnp.where(row < HEAD_DIM, qt, 0.0).astype(BF16)
    qbd_ref[:, T:] = jnp.where(row >= HEAD_DIM, qt, 0.0).astype(BF16)
    if online:
        m_ref[...] = jnp.full(m_ref.shape, NEG_BIG, F32)
        l_ref[...] = jnp.zeros(l_ref.shape, F32)
        acc1_ref[...] = jnp.zeros(acc1_ref.shape, F32)
        acc2_ref[...] = jnp.zeros(acc2_ref.shape, F32)

    H2 = T // 2

    def on_upper_half(v, fill):
        pad = jnp.full((1, H2), fill, F32)
        return jnp.concatenate([pad, v[:, :H2], pad, v[:, H2:]], axis=1)

    def block(j, nblk, first_bias=None, plain=0, half_last=False, first=False):
        n_main = nblk - 1 if half_last else nblk
        kb = k_ref[0, pl.ds(pl.multiple_of(j * TK, TK), nblk * TK), :]
        s = jnp.dot(kb[:n_main * TK], qbd_ref[...], preferred_element_type=F32)
        if first_bias is not None:
            nb = n_main - plain
            b = bias_ref[first_bias:first_bias + nb, 0].reshape(nb * TK, T)
            biased = s[plain * TK:] + jnp.concatenate([b, b], axis=1)
            s = biased if plain == 0 else jnp.concatenate([s[:plain * TK], biased], axis=0)
        vtb = jnp.concatenate([vt_ref[0, 0, j + t] for t in range(n_main)], axis=1)
        if half_last:
            q_half = jnp.concatenate([qbd_ref[:, H2:T], qbd_ref[:, T + H2:]], axis=1)
            b_last = bias_ref[first_bias + n_main - plain, 0][:, H2:]
            s_last = (jnp.dot(kb[n_main * TK:], q_half, preferred_element_type=F32)
                      + jnp.concatenate([b_last, b_last], axis=1))
            vt_last = vt_ref[0, 0, j + n_main]
        if not online:
            p = jnp.exp2(s)
            lsum = jnp.sum(p, axis=0, keepdims=True)
            pb = p.astype(BF16)
            pv1 = jnp.dot(vtb, pb[:, :T], preferred_element_type=F32)
            pv2 = jnp.dot(vtb, pb[:, T:], preferred_element_type=F32)
            if half_last:
                p_last = jnp.exp2(s_last)
                lsum = lsum + on_upper_half(jnp.sum(p_last, axis=0, keepdims=True), 0.0)
                pb_last = p_last.astype(BF16)
            acc1_ref[...] = pv1 if first else acc1_ref[...] + pv1
            acc2_ref[...] = pv2 if first else acc2_ref[...] + pv2
            l_ref[...] = lsum if first else l_ref[...] + lsum
            if half_last:
                acc1_ref[:, H2:] += jnp.dot(vt_last, pb_last[:, :H2], preferred_element_type=F32)
                acc2_ref[:, H2:] += jnp.dot(vt_last, pb_last[:, H2:], preferred_element_type=F32)
            return
        m_old = m_ref[...]
        cmax = jnp.max(s, axis=0, keepdims=True)
        if half_last:
            cmax = jnp.maximum(cmax, on_upper_half(jnp.max(s_last, axis=0, keepdims=True), NEG_BIG))
        m_new = jnp.maximum(m_old, cmax)
        alpha = jnp.exp2(m_old - m_new)
        p = jnp.exp2(s - m_new)
        lsum = jnp.sum(p, axis=0, keepdims=True)
        pb = p.astype(BF16)
        acc1_ref[...] = alpha[:, :T] * acc1_ref[...] + jnp.dot(vtb, pb[:, :T], preferred_element_type=F32)
        acc2_ref[...] = alpha[:, T:] * acc2_ref[...] + jnp.dot(vtb, pb[:, T:], preferred_element_type=F32)
        if half_last:
            m_half = jnp.concatenate([m_new[:, H2:T], m_new[:, T + H2:]], axis=1)
            p_last = jnp.exp2(s_last - m_half)
            lsum = lsum + on_upper_half(jnp.sum(p_last, axis=0, keepdims=True), 0.0)
            pb_last = p_last.astype(BF16)
            acc1_ref[:, H2:] += jnp.dot(vt_last, pb_last[:, :H2], preferred_element_type=F32)
            acc2_ref[:, H2:] += jnp.dot(vt_last, pb_last[:, H2:], preferred_element_type=F32)
        l_ref[...] = alpha * l_ref[...] + lsum
        m_ref[...] = m_new

    @pl.when(i > 0)
    def _():
        block(ratio * i - 2, ratio + 2, 0, plain=1, half_last=True, first=True)

    @pl.when(i == 0)
    def _():
        block(0, ratio, 1, half_last=True, first=True)

    n_far = jnp.maximum(ratio * i - 2, 0)

    def far_step(jj, carry):
        block(FAR_STEP * jj, FAR_STEP)
        return carry

    lax.fori_loop(0, n_far // FAR_STEP, far_step, 0)
    done = (n_far // FAR_STEP) * FAR_STEP
    size = FAR_STEP // 2
    while size >= 2:
        take = (n_far - done) >= size

        @pl.when(take)
        def _(done=done, size=size):
            block(done, size)

        done = done + jnp.where(take, size, 0)
        size //= 2

    lv = lamv_ref[...]
    lam = (jnp.exp(jnp.sum(lv[0:1] * lv[1:2], axis=1, keepdims=True))
           - jnp.exp(jnp.sum(lv[2:3] * lv[3:4], axis=1, keepdims=True)) + lam_init)
    l = l_ref[...]
    d = acc1_ref[...] / l[:, :T] - lam * (acc2_ref[...] / l[:, T:])
    ms = jnp.mean(d * d, axis=0, keepdims=True)
    y = d * lax.rsqrt(ms + EPS)
    o_ref[0] = (y.T * sg_ref[...]).astype(o_ref.dtype)


def _diff_attn(online, lam_init, q, k, vt, bias, lamv, sg):
    b, s, _ = q.shape
    T = Q_TILE
    TK = ATT_TILE
    nq = s // T
    return pl.pallas_call(
        functools.partial(_attn_body, lam_init, online),
        grid=(b, N_HEADS, nq),
        in_specs=[
            pl.BlockSpec(lamv.shape, lambda bi, h, i: (0, 0)),
            pl.BlockSpec((1, T, 2 * HEAD_DIM), lambda bi, h, i: (bi, i, h)),
            pl.BlockSpec((1, s, 2 * HEAD_DIM), lambda bi, h, i: (bi, 0, h)),
            pl.BlockSpec((1, 1, s // TK, 2 * HEAD_DIM, TK), lambda bi, h, i: (bi, h, 0, 0, 0)),
            pl.BlockSpec((bias.shape[0], 1, TK, T), lambda bi, h, i: (0, h, 0, 0)),
            pl.BlockSpec(sg.shape, lambda bi, h, i: (0, 0)),
        ],
        out_specs=pl.BlockSpec((1, T, 2 * HEAD_DIM), lambda bi, h, i: (bi, i, h)),
        out_shape=jax.ShapeDtypeStruct((b, s, DIFF_WIDTH), BF16),
        scratch_shapes=[pltpu.VMEM((2 * HEAD_DIM, 2 * T), BF16), pltpu.VMEM((1, 2 * T), F32),
                        pltpu.VMEM((1, 2 * T), F32), pltpu.VMEM((2 * HEAD_DIM, T), F32),
                        pltpu.VMEM((2 * HEAD_DIM, T), F32)],
        compiler_params=pltpu.CompilerParams(dimension_semantics=("arbitrary", "arbitrary", "arbitrary"),
                                             vmem_limit_bytes=VMEM_LIMIT),
        name="diff_attn_online" if online else "diff_attn",
    )(lamv, q, k, vt, bias, sg)


def _rel_bucket_np(n):
    max_exact = NUM_BUCKETS // 2
    nf = np.maximum(n, 1).astype(np.float32)
    large = max_exact + (np.log(nf / max_exact) / math.log(MAX_DISTANCE / max_exact)
                         * (NUM_BUCKETS - max_exact)).astype(np.int32)
    large = np.minimum(large, NUM_BUCKETS - 1)
    return np.where(n < max_exact, n, large)


def _bias_body(kinds, tab_ref, idx_ref, o_ref):
    h = pl.program_id(0)
    for t, tile_kinds in enumerate(kinds):
        for i, row_kinds in enumerate(tile_kinds):
            for j, kind in enumerate(row_kinds):
                rows, cols = slice(i * LANES, (i + 1) * LANES), slice(j * LANES, (j + 1) * LANES)
                if kind != "band":
                    fill = NEG_BIG if kind == "masked" else 0.0
                    o_ref[t, 0, rows, cols] = jnp.full((LANES, LANES), fill, F32)
                    continue
                idx = idx_ref[t, rows, cols]
                acc = jnp.zeros(idx.shape, F32)
                for b in range(NUM_BUCKETS):
                    acc = jnp.where(idx == b, tab_ref[b, h], acc)
                o_ref[t, 0, rows, cols] = jnp.where(idx < 0, NEG_BIG, acc)


def _bias_tiles(tab):
    T, TK = Q_TILE, ATT_TILE
    n_tiles = T // TK + 1
    c = np.arange(TK)[:, None]
    r = np.arange(T)[None, :]
    dist = [r - c + TK * (1 - m) for m in range(n_tiles)]
    idx = jnp.asarray(np.stack([np.where(d >= 0, _rel_bucket_np(np.maximum(d, 0)), -1)
                                for d in dist]).astype(np.int32))

    def kind(d):
        return "masked" if (d < 0).all() else "zero" if (d >= MAX_DISTANCE).all() else "band"

    kinds = tuple(tuple(tuple(kind(d[i:i + LANES, j:j + LANES]) for j in range(0, T, LANES))
                        for i in range(0, TK, LANES)) for d in dist)
    return pl.pallas_call(
        functools.partial(_bias_body, kinds),
        grid=(N_HEADS,),
        in_specs=[pl.BlockSpec(memory_space=pltpu.SMEM),
                  pl.BlockSpec((n_tiles, TK, T), lambda h: (0, 0, 0))],
        out_specs=pl.BlockSpec((n_tiles, 1, TK, T), lambda h: (0, h, 0, 0)),
        out_shape=jax.ShapeDtypeStruct((n_tiles, N_HEADS, TK, T), F32),
        name="bias_tiles",
    )(tab, idx)


def kernel(x, rel_bias, ffn1_norm, ffn1_w_gate, ffn1_w_up, ffn1_w_down, mix_norm, w_in, conv_w, conv_b,
           conv_norm, q_norm, k_norm, lambda_q1, lambda_k1, lambda_q2, lambda_k2, subln_norm, w_out,
           ffn2_norm, ffn2_w_gate, ffn2_w_up, ffn2_w_down):
    B, S, _ = x.shape
    depth = w_in.shape[0]
    assert S % Q_TILE == 0 and Q_TILE == 2 * ATT_TILE and ROW_TILE == ATT_TILE and ATT_TILE >= MAX_DISTANCE

    grp = np.arange(GROUP_MM) // HEAD_DIM
    gsum = jnp.asarray(grp[:, None] == grp[None, :], BF16)
    tab = (rel_bias - rel_bias[NUM_BUCKETS - 1:]).astype(F32) * LOG2E
    bias = _bias_tiles(tab)
    row2 = lambda a: a.reshape(1, -1).astype(F32)

    xf = x.reshape(B * S, D_MODEL)
    for l in range(depth):
        lam_init = 0.8 - 0.6 * math.exp(-0.3 * l)
        qg = q_norm[l].astype(F32) * (LOG2E / math.sqrt(HEAD_DIM))
        kg = k_norm[l].astype(F32)
        qkg = jnp.concatenate([jnp.tile(qg, 2 * N_HEADS), jnp.tile(kg, 2 * N_HEADS)]).reshape(1, -1)
        score_bound = (HEAD_DIM * 1.02) * jnp.max(jnp.abs(qg)) * jnp.max(jnp.abs(kg)) + jnp.max(jnp.abs(tab))
        cw8 = jnp.broadcast_to(conv_w[l].astype(F32)[:, None, :], (CONV_K, SUBLANES, CONV_CH))
        x1, c, q, k, vt = _ffn_proj(xf, row2(ffn1_norm[l]), ffn1_w_gate[l].astype(BF16),
                                   ffn1_w_up[l].astype(BF16), ffn1_w_down[l].astype(BF16),
                                   row2(mix_norm[l]), w_in[l].astype(BF16), gsum, qkg, cw8,
                                   row2(conv_b[l]), row2(conv_norm[l]), S)
        lamv = jnp.stack([lambda_q1[l], lambda_k1[l], lambda_q2[l], lambda_k2[l]]).astype(F32)
        sg = row2(subln_norm[l]) * (1.0 - lam_init)
        att = lax.cond(score_bound <= SAFE_EXP2_BOUND,
                       functools.partial(_diff_attn, False, lam_init),
                       functools.partial(_diff_attn, True, lam_init),
                       q.reshape(B, S, DIFF_WIDTH), k.reshape(B, S, DIFF_WIDTH), vt, bias, lamv, sg)
        xf = _mix_ffn(x1, c, att.reshape(B * S, DIFF_WIDTH), w_out[l, :CONV_CH].astype(BF16),
                      w_out[l, CONV_CH:].astype(BF16), row2(ffn2_norm[l]), ffn2_w_gate[l].astype(BF16),
                      ffn2_w_up[l].astype(BF16), ffn2_w_down[l].astype(BF16))
    return xf.reshape(B, S, D_MODEL)
```
